```python
import jax, jax.numpy as jnp
from jax import lax
import numpy as np

D_MODEL = 2048
BATCH = 4
SEQ = 4096
DEPTH = 4

N_MIXERS = 2
N_HEADS = 16
HEAD_DIM = D_MODEL // N_HEADS
D_FF = 4 * D_MODEL
CONV_WIDTH = 3
Q_BLOCK = 128
N_ATTN_LAYERS = (DEPTH + 1) // 2
N_CONV_LAYERS = DEPTH // 2
N_MOD = 6
DEEPNORM_ALPHA = (2.0 * DEPTH) ** 0.25
DEEPNORM_BETA = (8.0 * DEPTH) ** -0.25
LN_EPS = 1e-5
MOD_INIT_SCALE = 0.1

kernel_name = "hybrid_stickbreak_shortconv_deepnorm_adaln"


def layer_norm(x, g, b):
    xf = x.astype(jnp.float32)
    mu = jnp.mean(xf, axis=-1, keepdims=True)
    xc = xf - mu
    var = jnp.mean(xc * xc, axis=-1, keepdims=True)
    y = xc * lax.rsqrt(var + LN_EPS)
    return (y * g.astype(jnp.float32) + b.astype(jnp.float32)).astype(x.dtype)


def stick_breaking_attention(h, w_qkv, w_o):
    bsz, seq, _ = h.shape
    qkv = h @ w_qkv
    q, k, v = jnp.split(qkv, 3, axis=-1)
    q = q.reshape(bsz, seq, N_HEADS, HEAD_DIM)
    k = k.reshape(bsz, seq, N_HEADS, HEAD_DIM)
    v = v.reshape(bsz, seq, N_HEADS, HEAD_DIM)
    scale = HEAD_DIM ** -0.5
    outs = []
    for blk in range(seq // Q_BLOCK):
        start = blk * Q_BLOCK
        end = start + Q_BLOCK
        qb = q[:, start:end].astype(jnp.float32)
        kb = k[:, :end].astype(jnp.float32)
        vb = v[:, :end].astype(jnp.float32)
        z = jnp.einsum('bqhd,bkhd->bhqk', qb, kb) * scale
        t_idx = start + jnp.arange(Q_BLOCK)[:, None]
        s_idx = jnp.arange(end)[None, :]
        causal = s_idx < t_idx
        log_beta = jax.nn.log_sigmoid(z)
        log_1m = jnp.where(causal, jax.nn.log_sigmoid(-z), 0.0)
        log_stick = lax.cumsum(log_1m, axis=3, reverse=True) - log_1m
        a = jnp.where(causal, jnp.exp(log_beta + log_stick), 0.0)
        outs.append(jnp.einsum('bhqk,bkhd->bqhd', a, vb))
    o = jnp.concatenate(outs, axis=1).reshape(bsz, seq, D_MODEL).astype(h.dtype)
    return o @ w_o


def short_conv_mixer(h, w_in, conv_w, conv_b, w_out):
    bcu = h @ w_in
    b_gate, c_gate, u = jnp.split(bcu, 3, axis=-1)
    gated = c_gate * u
    y = lax.conv_general_dilated(
        gated, conv_w[:, None, :].astype(gated.dtype),
        window_strides=(1,), padding=[(CONV_WIDTH - 1, 0)],
        dimension_numbers=('NWC', 'WIO', 'NWC'),
        feature_group_count=D_MODEL) + conv_b
    return (b_gate * y) @ w_out


def sq_relu_mlp(h, w1, b1, w2, b2):
    return jnp.square(jax.nn.relu(h @ w1 + b1)) @ w2 + b2


def setup_inputs(seed: int = 0) -> dict:
    key = jax.random.key(seed)
    ks = jax.random.split(key, 20)
    d, f = D_MODEL, D_FF
    nrm = jax.random.normal
    x = nrm(ks[0], (BATCH, SEQ, d), jnp.float32)
    c = nrm(ks[1], (BATCH, d), jnp.float32)
    mod_w = nrm(ks[2], (DEPTH, d, N_MOD * d), jnp.float32) * (d ** -0.5) * MOD_INIT_SCALE
    mod_b = nrm(ks[3], (DEPTH, N_MOD * d), jnp.float32) * 0.01
    ln_g = 1.0 + 0.02 * nrm(ks[4], (DEPTH, 2, d), jnp.float32)
    ln_b = 0.02 * nrm(ks[5], (DEPTH, 2, d), jnp.float32)
    v_col_scale = jnp.concatenate([jnp.ones((2 * d,), jnp.float32),
                                   jnp.full((d,), DEEPNORM_BETA, jnp.float32)])
    attn_w_qkv = nrm(ks[6], (N_ATTN_LAYERS, d, 3 * d), jnp.float32) * (d ** -0.5) * v_col_scale
    attn_w_o = nrm(ks[7], (N_ATTN_LAYERS, d, d), jnp.float32) * (d ** -0.5) * DEEPNORM_BETA
    conv_w_in = nrm(ks[8], (N_CONV_LAYERS, d, 3 * d), jnp.float32) * (d ** -0.5)
    conv_w = nrm(ks[9], (N_CONV_LAYERS, CONV_WIDTH, d), jnp.float32) * (CONV_WIDTH ** -0.5)
    conv_b = nrm(ks[10], (N_CONV_LAYERS, d), jnp.float32) * 0.01
    conv_w_out = nrm(ks[11], (N_CONV_LAYERS, d, d), jnp.float32) * (d ** -0.5) * DEEPNORM_BETA
    mlp_w1 = nrm(ks[12], (DEPTH, d, f), jnp.float32) * (d ** -0.5) * DEEPNORM_BETA
    mlp_b1 = nrm(ks[13], (DEPTH, f), jnp.float32) * 0.01
    mlp_w2 = nrm(ks[14], (DEPTH, f, d), jnp.float32) * (f ** -0.5) * DEEPNORM_BETA
    mlp_b2 = nrm(ks[15], (DEPTH, d), jnp.float32) * 0.01
    return {"x": x, "c": c, "mod_w": mod_w, "mod_b": mod_b, "ln_g": ln_g, "ln_b": ln_b,
            "attn_w_qkv": attn_w_qkv, "attn_w_o": attn_w_o,
            "conv_w_in": conv_w_in, "conv_w": conv_w, "conv_b": conv_b, "conv_w_out": conv_w_out,
            "mlp_w1": mlp_w1, "mlp_b1": mlp_b1, "mlp_w2": mlp_w2, "mlp_b2": mlp_b2}


def reference(x, c, mod_w, mod_b, ln_g, ln_b, attn_w_qkv, attn_w_o,
              conv_w_in, conv_w, conv_b, conv_w_out, mlp_w1, mlp_b1, mlp_w2, mlp_b2):
    cond = jax.nn.silu(c)
    for i in range(DEPTH):
        mod = (cond @ mod_w[i] + mod_b[i])[:, None, :]
        sh_m, sc_m, g_m, sh_f, sc_f, g_f = jnp.split(mod, N_MOD, axis=-1)
        h = x * (1.0 + sc_m) + sh_m
        if i % N_MIXERS == 0:
            j = i // N_MIXERS
            y = stick_breaking_attention(h, attn_w_qkv[j], attn_w_o[j])
        else:
            j = i // N_MIXERS
            y = short_conv_mixer(h, conv_w_in[j], conv_w[j], conv_b[j], conv_w_out[j])
        x = layer_norm(DEEPNORM_ALPHA * x + (1.0 + g_m) * y, ln_g[i, 0], ln_b[i, 0])
        h = x * (1.0 + sc_f) + sh_f
        y = sq_relu_mlp(h, mlp_w1[i], mlp_b1[i], mlp_w2[i], mlp_b2[i])
        x = layer_norm(DEEPNORM_ALPHA * x + (1.0 + g_f) * y, ln_g[i, 1], ln_b[i, 1])
    return x
```

```python
import functools

import jax
import jax.numpy as jnp
from jax import lax
from jax.experimental import pallas as pl
from jax.experimental.pallas import tpu as pltpu

N_HEADS = 16
N_MOD = 6
CONV_WIDTH = 3
LN_EPS = 1e-5
SUBLANES = 8
VMEM_LIMIT_BYTES = 56 * 1024 * 1024

F32 = jnp.float32
BF16 = jnp.bfloat16


def _params(semantics):
    return pltpu.CompilerParams(dimension_semantics=semantics, vmem_limit_bytes=VMEM_LIMIT_BYTES)


def _modulation_kernel(c_ref, w_ref, b_ref, o_ref):
    c = c_ref[...]
    cond = c * jax.nn.sigmoid(c)
    o_ref[0] = jnp.dot(cond, w_ref[0], preferred_element_type=F32,
                       precision=lax.Precision.HIGHEST) + b_ref[0]


def _modulation(c, mod_w, mod_b, *, tn=1024):
    depth, d, n = mod_w.shape
    bsz = c.shape[0]
    rows = -(-bsz // SUBLANES) * SUBLANES
    c_pad = jnp.zeros((rows, d), F32).at[:bsz].set(c)
    out = pl.pallas_call(
        _modulation_kernel,
        grid=(depth, n // tn),
        in_specs=[pl.BlockSpec((rows, d), lambda i, j: (0, 0)),
                  pl.BlockSpec((1, d, tn), lambda i, j: (i, 0, j)),
                  pl.BlockSpec((1, 1, tn), lambda i, j: (i, 0, j))],
        out_specs=pl.BlockSpec((1, rows, tn), lambda i, j: (i, 0, j)),
        out_shape=jax.ShapeDtypeStruct((depth, rows, n), F32),
        compiler_params=_params(("arbitrary", "arbitrary")),
        name="modulation",
    )(c_pad, mod_w, mod_b.reshape(depth, 1, n))
    return out[:, :bsz]


def _mod_matmul_kernel(x_ref, sc_ref, sh_ref, w_ref, b_ref, o_ref, h_ref, *, squared_relu):
    @pl.when(pl.program_id(1) == 0)
    def _():
        h_ref[...] = (x_ref[...] * (1.0 + sc_ref[0]) + sh_ref[0]).astype(BF16)

    y = jnp.dot(h_ref[...], w_ref[...], preferred_element_type=F32) + b_ref[...]
    if squared_relu:
        y = jnp.square(jnp.maximum(y, 0.0))
    o_ref[...] = y.astype(o_ref.dtype)


def _mod_matmul(x, mod3, sc_idx, sh_idx, w, b, *, seq, squared_relu, tm=1024, tn=512):
    m, d = x.shape
    n = w.shape[1]
    tm = min(tm, seq)
    per_seq = seq // tm

    def vec(j):
        return pl.BlockSpec((1, 1, d), lambda i, _: ((i // per_seq) * N_MOD + j, 0, 0))

    return pl.pallas_call(
        functools.partial(_mod_matmul_kernel, squared_relu=squared_relu),
        grid=(m // tm, n // tn),
        in_specs=[pl.BlockSpec((tm, d), lambda i, j: (i, 0)),
                  vec(sc_idx), vec(sh_idx),
                  pl.BlockSpec((d, tn), lambda i, j: (0, j)),
                  pl.BlockSpec((1, tn), lambda i, j: (0, j))],
        out_specs=pl.BlockSpec((tm, tn), lambda i, j: (i, j)),
        out_shape=jax.ShapeDtypeStruct((m, n), BF16),
        scratch_shapes=[pltpu.VMEM((tm, d), BF16)],
        compiler_params=_params(("arbitrary", "arbitrary")),
        name="mod_matmul",
    )(x, mod3, mod3, w, b.reshape(1, n))


def _residual_ln_rows(y_ref, bias_ref, x_ref, gate_ref, lng_ref, lnb_ref, o_ref, *, alpha, rows):
    tm = o_ref.shape[0]
    gate = 1.0 + gate_ref[0]
    bias = bias_ref[...]
    lng = lng_ref[...]
    lnb = lnb_ref[...]

    def chunk(i, carry):
        sl = pl.ds(pl.multiple_of(i * rows, rows), rows)
        r = alpha * x_ref[sl, :] + gate * (y_ref[sl, :] + bias)
        mu = jnp.mean(r, axis=-1, keepdims=True)
        rc = r - mu
        var = jnp.mean(rc * rc, axis=-1, keepdims=True)
        o_ref[sl, :] = rc * lax.rsqrt(var + LN_EPS) * lng + lnb
        return carry

    lax.fori_loop(0, tm // rows, chunk, 0)


def _matmul_residual_ln_kernel(a_ref, w_ref, b_ref, x_ref, gate_ref, lng_ref, lnb_ref, o_ref,
                               acc_ref, *, alpha, nk):
    k = pl.program_id(1)
    prod = jnp.dot(a_ref[...], w_ref[...], preferred_element_type=F32)
    if nk == 1:
        acc_ref[...] = prod
    else:
        @pl.when(k == 0)
        def _():
            acc_ref[...] = prod

        @pl.when(k > 0)
        def _():
            acc_ref[...] += prod

    @pl.when(k == nk - 1)
    def _():
        _residual_ln_rows(acc_ref, b_ref, x_ref, gate_ref, lng_ref, lnb_ref, o_ref,
                          alpha=alpha, rows=32)


def _matmul_residual_ln(a, w, b, x, mod3, gate_idx, ln_g, ln_b, *, seq, alpha, tm=512, tk=1024):
    m, kdim = a.shape
    d = w.shape[1]
    tm = min(tm, seq)
    tk = min(tk, kdim)
    per_seq = seq // tm
    nk = kdim // tk
    row = lambda i, k: (0, 0)
    return pl.pallas_call(
        functools.partial(_matmul_residual_ln_kernel, alpha=alpha, nk=nk),
        grid=(m // tm, nk),
        in_specs=[pl.BlockSpec((tm, tk), lambda i, k: (i, k)),
                  pl.BlockSpec((tk, d), lambda i, k: (k, 0)),
                  pl.BlockSpec((1, d), row),
                  pl.BlockSpec((tm, d), lambda i, k: (i, 0)),
                  pl.BlockSpec((1, 1, d), lambda i, k: ((i // per_seq) * N_MOD + gate_idx, 0, 0)),
                  pl.BlockSpec((1, d), row),
                  pl.BlockSpec((1, d), row)],
        out_specs=pl.BlockSpec((tm, d), lambda i, k: (i, 0)),
        out_shape=jax.ShapeDtypeStruct((m, d), F32),
        scratch_shapes=[pltpu.VMEM((tm, d), F32)],
        compiler_params=_params(("arbitrary", "arbitrary")),
        name="matmul_residual_ln",
    )(a, w, b.reshape(1, d), x, mod3, ln_g.reshape(1, d), ln_b.reshape(1, d))


def _conv_residual_ln_kernel(bg_ref, cg_ref, u_ref, cgh_ref, uh_ref, cw_ref, cb_ref, w_ref, zb_ref,
                             x_ref, gate_ref, lng_ref, lnb_ref, o_ref, p_ref, y_ref,
                             *, alpha, per_seq, tc):
    tm, d = o_ref.shape
    first = (pl.program_id(0) % per_seq) == 0
    for c0 in range(0, d, tc):
        cs = slice(c0, c0 + tc)
        gated = cg_ref[:, cs].astype(F32) * u_ref[:, cs].astype(F32)
        halo = cgh_ref[:, cs].astype(F32) * uh_ref[:, cs].astype(F32)
        halo = jnp.where(first, 0.0, halo)
        g = jnp.concatenate([halo, gated], axis=0)
        g1 = pltpu.roll(g, 1, 0)[SUBLANES:]
        g2 = pltpu.roll(g, 2, 0)[SUBLANES:]
        conv = (cw_ref[0:1, cs] * g2 + cw_ref[1:2, cs] * g1 + cw_ref[2:3, cs] * gated
                + cb_ref[:, cs])
        p_ref[:, cs] = (bg_ref[:, cs].astype(F32) * conv).astype(BF16)
    y_ref[...] = jnp.dot(p_ref[...], w_ref[...], preferred_element_type=F32)
    _residual_ln_rows(y_ref, zb_ref, x_ref, gate_ref, lng_ref, lnb_ref, o_ref, alpha=alpha, rows=32)


def _conv_residual_ln(bcu, conv_w, conv_b, w_out, x, mod3, gate_idx, ln_g, ln_b, *, seq, alpha,
                      tm=512, tc=512):
    m, d = x.shape
    tm = min(tm, seq)
    tc = min(tc, d)
    per_seq = seq // tm
    hb = tm // SUBLANES
    row = lambda i: (0, 0)
    halo = lambda col: pl.BlockSpec((SUBLANES, d), lambda i: (jnp.maximum(i * hb - 1, 0), col))
    return pl.pallas_call(
        functools.partial(_conv_residual_ln_kernel, alpha=alpha, per_seq=per_seq, tc=tc),
        grid=(m // tm,),
        in_specs=[pl.BlockSpec((tm, d), lambda i: (i, 0)),
                  pl.BlockSpec((tm, d), lambda i: (i, 1)),
                  pl.BlockSpec((tm, d), lambda i: (i, 2)),
                  halo(1), halo(2),
                  pl.BlockSpec((CONV_WIDTH, d), row),
                  pl.BlockSpec((1, d), row),
                  pl.BlockSpec((d, d), row),
                  pl.BlockSpec((1, d), row),
                  pl.BlockSpec((tm, d), lambda i: (i, 0)),
                  pl.BlockSpec((1, 1, d), lambda i: ((i // per_seq) * N_MOD + gate_idx, 0, 0)),
                  pl.BlockSpec((1, d), row),
                  pl.BlockSpec((1, d), row)],
        out_specs=pl.BlockSpec((tm, d), lambda i: (i, 0)),
        out_shape=jax.ShapeDtypeStruct((m, d), F32),
        scratch_shapes=[pltpu.VMEM((tm, d), BF16), pltpu.VMEM((tm, d), F32)],
        compiler_params=_params(("arbitrary",)),
        name="conv_residual_ln",
    )(bcu, bcu, bcu, bcu, bcu, conv_w, conv_b.reshape(1, d), w_out, jnp.zeros((1, d), F32),
      x, mod3, ln_g.reshape(1, d), ln_b.reshape(1, d))


def _attention_kernel(q_ref, k_ref, v_ref, o_ref, *, t, scale):
    seq, dh = q_ref.shape
    row = lax.broadcasted_iota(jnp.int32, (t, t), 0)
    col = lax.broadcasted_iota(jnp.int32, (t, t), 1)
    causal = col < row
    later = jnp.where(row > col, 1.0, 0.0).astype(BF16)

    def rows(i):
        return pl.ds(pl.multiple_of(i * t, t), t)

    def scores(q, j):
        z = lax.dot_general(q, k_ref[rows(j), :], (((1,), (1,)), ((), ())),
                            preferred_element_type=F32)
        log_beta = jnp.minimum(z, 0.0) - jnp.log(1.0 + jnp.exp(-jnp.abs(z)))
        return z, log_beta

    def accumulate(j, log_beta, l1m, mask, carry, acc):
        hi = l1m.astype(BF16)
        lo = (l1m - hi.astype(F32)).astype(BF16)
        stick = (jnp.dot(hi, later, preferred_element_type=F32)
                 + jnp.dot(lo, later, preferred_element_type=F32))
        a = jnp.exp(log_beta + stick + carry)
        if mask is not None:
            a = jnp.where(mask, a, 0.0)
        acc = acc + jnp.dot(a.astype(BF16), v_ref[rows(j), :], preferred_element_type=F32)
        carry = carry + jnp.sum(l1m, axis=-1, keepdims=True)
        return carry, acc

    def q_tile(qi, _):
        q = (q_ref[rows(qi), :].astype(F32) * scale).astype(BF16)
        z, log_beta = scores(q, qi)
        l1m = jnp.where(causal, log_beta - z, 0.0)
        carry, acc = accumulate(qi, log_beta, l1m, causal,
                                jnp.zeros((t, 1), F32), jnp.zeros((t, dh), F32))

        def key_block(i, state):
            j = qi - 1 - i
            z, log_beta = scores(q, j)
            return accumulate(j, log_beta, log_beta - z, None, *state)

        carry, acc = lax.fori_loop(0, qi, key_block, (carry, acc))
        o_ref[rows(qi), :] = acc.astype(o_ref.dtype)
        return 0

    lax.fori_loop(0, seq // t, q_tile, 0)


def _attention(qkv, *, bsz, seq, n_heads, t=256):
    m, d3 = qkv.shape
    d = d3 // 3
    dh = d // n_heads
    t = min(t, seq)
    return pl.pallas_call(
        functools.partial(_attention_kernel, t=t, scale=dh ** -0.5),
        grid=(bsz, n_heads),
        in_specs=[pl.BlockSpec((seq, dh), lambda b, h: (b, h)),
                  pl.BlockSpec((seq, dh), lambda b, h: (b, n_heads + h)),
                  pl.BlockSpec((seq, dh), lambda b, h: (b, 2 * n_heads + h))],
        out_specs=pl.BlockSpec((seq, dh), lambda b, h: (b, h)),
        out_shape=jax.ShapeDtypeStruct((m, d), BF16),
        compiler_params=_params(("arbitrary", "arbitrary")),
        name="stickbreak_attention",
    )(qkv, qkv, qkv)


def _trunk(x, c, mod_w, mod_b, ln_g, ln_b, attn_w_qkv, attn_w_o, conv_w_in, conv_w, conv_b,
           conv_w_out, mlp_w1, mlp_b1, mlp_w2, mlp_b2, *, n_heads):
    bsz, seq, d = x.shape
    depth = mod_w.shape[0]
    alpha = (2.0 * depth) ** 0.25
    xf = x.reshape(bsz * seq, d)
    mod = _modulation(c, mod_w, mod_b)
    zeros3 = jnp.zeros((3 * d,), F32)
    zeros1 = jnp.zeros((d,), F32)
    for i in range(depth):
        mod3 = mod[i].reshape(bsz * N_MOD, 1, d)
        j = i // 2
        if i % 2 == 0:
            qkv = _mod_matmul(xf, mod3, 1, 0, attn_w_qkv[j].astype(BF16), zeros3, seq=seq,
                              squared_relu=False)
            o = _attention(qkv, bsz=bsz, seq=seq, n_heads=n_heads)
            xf = _matmul_residual_ln(o, attn_w_o[j].astype(BF16), zeros1, xf, mod3, 2,
                                     ln_g[i, 0], ln_b[i, 0], seq=seq, alpha=alpha, tk=d)
        else:
            bcu = _mod_matmul(xf, mod3, 1, 0, conv_w_in[j].astype(BF16), zeros3, seq=seq,
                              squared_relu=False)
            xf = _conv_residual_ln(bcu, conv_w[j], conv_b[j], conv_w_out[j].astype(BF16), xf, mod3, 2,
                                   ln_g[i, 0], ln_b[i, 0], seq=seq, alpha=alpha)
        hid = _mod_matmul(xf, mod3, 4, 3, mlp_w1[i].astype(BF16), mlp_b1[i], seq=seq,
                          squared_relu=True)
        xf = _matmul_residual_ln(hid, mlp_w2[i].astype(BF16), mlp_b2[i], xf, mod3, 5,
                                 ln_g[i, 1], ln_b[i, 1], seq=seq, alpha=alpha)
    return xf.reshape(bsz, seq, d)


def kernel(x, c, mod_w, mod_b, ln_g, ln_b, attn_w_qkv, attn_w_o, conv_w_in, conv_w, conv_b,
           conv_w_out, mlp_w1, mlp_b1, mlp_w2, mlp_b2):
    return _trunk(x, c, mod_w, mod_b, ln_g, ln_b, attn_w_qkv, attn_w_o, conv_w_in, conv_w, conv_b,
                  conv_w_out, mlp_w1, mlp_b1, mlp_w2, mlp_b2, n_heads=N_HEADS)
```

```python
import functools

import jax
import jax.numpy as jnp
from jax import lax
from jax.experimental import pallas as pl
from jax.experimental.pallas import tpu as pltpu

N_HEADS = 16
N_MOD = 6
CONV_WIDTH = 3
LN_EPS = 1e-5
LOG2_E = 1.4426950408889634
SUBLANES = 8
VMEM_LIMIT_BYTES = 56 * 1024 * 1024
CAST_BLOCK_ELEMS = 2 * 1024 * 1024

F32 = jnp.float32
BF16 = jnp.bfloat16


def _params(semantics):
    return pltpu.CompilerParams(dimension_semantics=semantics, vmem_limit_bytes=VMEM_LIMIT_BYTES)


def _cast_kernel(w_ref, o_ref):
    o_ref[...] = w_ref[...].astype(o_ref.dtype)


def _to_bf16(w):
    layers, k, n = w.shape
    tk = k
    while tk * n > CAST_BLOCK_ELEMS and tk % 32 == 0:
        tk //= 2
    return pl.pallas_call(
        _cast_kernel,
        grid=(layers, k // tk),
        in_specs=[pl.BlockSpec((1, tk, n), lambda l, i: (l, i, 0))],
        out_specs=pl.BlockSpec((1, tk, n), lambda l, i: (l, i, 0)),
        out_shape=jax.ShapeDtypeStruct(w.shape, BF16),
        compiler_params=_params(("arbitrary", "arbitrary")),
        name="to_bf16",
    )(w)


def _modulation_kernel(c_ref, w_ref, b_ref, o_ref):
    c = c_ref[...]
    cond = c * jax.nn.sigmoid(c)
    o_ref[0] = jnp.dot(cond, w_ref[0], preferred_element_type=F32,
                       precision=lax.Precision.HIGHEST) + b_ref[0]


def _modulation(c, mod_w, mod_b, *, tn=1024):
    depth, d, n = mod_w.shape
    bsz = c.shape[0]
    rows = -(-bsz // SUBLANES) * SUBLANES
    c_pad = jnp.zeros((rows, d), F32).at[:bsz].set(c)
    out = pl.pallas_call(
        _modulation_kernel,
        grid=(depth, n // tn),
        in_specs=[pl.BlockSpec((rows, d), lambda i, j: (0, 0)),
                  pl.BlockSpec((1, d, tn), lambda i, j: (i, 0, j)),
                  pl.BlockSpec((1, 1, tn), lambda i, j: (i, 0, j))],
        out_specs=pl.BlockSpec((1, rows, tn), lambda i, j: (i, 0, j)),
        out_shape=jax.ShapeDtypeStruct((depth, rows, n), F32),
        compiler_params=_params(("arbitrary", "arbitrary")),
        name="modulation",
    )(c_pad, mod_w, mod_b.reshape(depth, 1, n))
    return out[:, :bsz]


def _mod_vec(d, per_seq, j):
    return pl.BlockSpec((1, 1, d), lambda i, *_: ((i // per_seq) * N_MOD + j, 0, 0))


def _mod_matmul_kernel(x_ref, sc_ref, sh_ref, w_ref, o_ref, h_ref):
    @pl.when(pl.program_id(1) == 0)
    def _():
        h_ref[...] = (x_ref[...] * (1.0 + sc_ref[0]) + sh_ref[0]).astype(BF16)

    o_ref[...] = jnp.dot(h_ref[...], w_ref[...], preferred_element_type=F32).astype(o_ref.dtype)


def _mod_matmul(x, mod3, w, layer, *, seq, tm=1024, tn=512):
    m, d = x.shape
    n = w.shape[2]
    tm = min(tm, seq)
    per_seq = seq // tm
    return pl.pallas_call(
        _mod_matmul_kernel,
        grid=(m // tm, n // tn),
        in_specs=[pl.BlockSpec((tm, d), lambda i, j: (i, 0)),
                  _mod_vec(d, per_seq, 1), _mod_vec(d, per_seq, 0),
                  pl.BlockSpec((None, d, tn), lambda i, j: (layer, 0, j))],
        out_specs=pl.BlockSpec((tm, tn), lambda i, j: (i, j)),
        out_shape=jax.ShapeDtypeStruct((m, n), BF16),
        scratch_shapes=[pltpu.VMEM((tm, d), BF16)],
        compiler_params=_params(("arbitrary", "arbitrary")),
        name="mod_matmul",
    )(x, mod3, mod3, w)


def _residual_ln_rows(y_ref, bias, x_ref, gate_ref, lng_ref, lnb_ref, o_ref, *, alpha, rows=32):
    tm = o_ref.shape[0]
    gate = 1.0 + gate_ref[0]
    lng = lng_ref[...]
    lnb = lnb_ref[...]

    def chunk(i, carry):
        sl = pl.ds(pl.multiple_of(i * rows, rows), rows)
        y = y_ref[sl, :]
        if bias is not None:
            y = y + bias
        r = alpha * x_ref[sl, :] + gate * y
        mu = jnp.mean(r, axis=-1, keepdims=True)
        rc = r - mu
        var = jnp.mean(rc * rc, axis=-1, keepdims=True)
        o_ref[sl, :] = rc * lax.rsqrt(var + LN_EPS) * lng + lnb
        return carry

    lax.fori_loop(0, tm // rows, chunk, 0)


def _proj_residual_ln_kernel(a_ref, w_ref, x_ref, gate_ref, lng_ref, lnb_ref, o_ref, y_ref,
                             *, alpha):
    y_ref[...] = jnp.dot(a_ref[...], w_ref[...], preferred_element_type=F32)
    _residual_ln_rows(y_ref, None, x_ref, gate_ref, lng_ref, lnb_ref, o_ref, alpha=alpha)


def _proj_residual_ln(a, w, layer, x, mod3, ln_g, ln_b, *, seq, alpha, tm=512):
    m, d = x.shape
    tm = min(tm, seq)
    per_seq = seq // tm
    row = lambda i: (0, 0)
    return pl.pallas_call(
        functools.partial(_proj_residual_ln_kernel, alpha=alpha),
        grid=(m // tm,),
        in_specs=[pl.BlockSpec((tm, d), lambda i: (i, 0)),
                  pl.BlockSpec((None, d, d), lambda i: (layer, 0, 0)),
                  pl.BlockSpec((tm, d), lambda i: (i, 0)),
                  _mod_vec(d, per_seq, 2),
                  pl.BlockSpec((1, d), row),
                  pl.BlockSpec((1, d), row)],
        out_specs=pl.BlockSpec((tm, d), lambda i: (i, 0)),
        out_shape=jax.ShapeDtypeStruct((m, d), F32),
        scratch_shapes=[pltpu.VMEM((tm, d), F32)],
        compiler_params=_params(("arbitrary",)),
        name="proj_residual_ln",
    )(a, w, x, mod3, ln_g.reshape(1, d), ln_b.reshape(1, d))


def _mlp_kernel(x_ref, sc_ref, sh_ref, w1_ref, b1_ref, w2_ref, b2_ref, gate_ref, lng_ref, lnb_ref,
                o_ref, h_ref, acc_ref, *, alpha, nf):
    f = pl.program_id(1)

    @pl.when(f == 0)
    def _():
        h_ref[...] = (x_ref[...] * (1.0 + sc_ref[0]) + sh_ref[0]).astype(BF16)

    hid = jnp.dot(h_ref[...], w1_ref[...], preferred_element_type=F32) + b1_ref[...]
    hid = jnp.square(jnp.maximum(hid, 0.0)).astype(BF16)
    prod = jnp.dot(hid, w2_ref[...], preferred_element_type=F32)

    @pl.when(f == 0)
    def _():
        acc_ref[...] = prod

    @pl.when(f > 0)
    def _():
        acc_ref[...] += prod

    @pl.when(f == nf - 1)
    def _():
        _residual_ln_rows(acc_ref, b2_ref[...], x_ref, gate_ref, lng_ref, lnb_ref, o_ref,
                          alpha=alpha)


def _mlp(x, mod3, w1, b1, w2, b2, layer, ln_g, ln_b, *, seq, alpha, tm=512, tf=1024):
    m, d = x.shape
    dff = w1.shape[2]
    tm = min(tm, seq)
    tf = min(tf, dff)
    per_seq = seq // tm
    row = lambda i, f: (0, 0)
    return pl.pallas_call(
        functools.partial(_mlp_kernel, alpha=alpha, nf=dff // tf),
        grid=(m // tm, dff // tf),
        in_specs=[pl.BlockSpec((tm, d), lambda i, f: (i, 0)),
                  _mod_vec(d, per_seq, 4), _mod_vec(d, per_seq, 3),
                  pl.BlockSpec((None, d, tf), lambda i, f: (layer, 0, f)),
                  pl.BlockSpec((1, tf), lambda i, f: (0, f)),
                  pl.BlockSpec((None, tf, d), lambda i, f: (layer, f, 0)),
                  pl.BlockSpec((1, d), row),
                  _mod_vec(d, per_seq, 5),
                  pl.BlockSpec((1, d), row),
                  pl.BlockSpec((1, d), row)],
        out_specs=pl.BlockSpec((tm, d), lambda i, f: (i, 0)),
        out_shape=jax.ShapeDtypeStruct((m, d), F32),
        scratch_shapes=[pltpu.VMEM((tm, d), BF16), pltpu.VMEM((tm, d), F32)],
        compiler_params=_params(("arbitrary", "arbitrary")),
        name="mlp",
    )(x, mod3, mod3, w1, b1.reshape(1, dff), w2, b2.reshape(1, d), mod3,
      ln_g.reshape(1, d), ln_b.reshape(1, d))


def _conv_residual_ln_kernel(bg_ref, cg_ref, u_ref, cgh_ref, uh_ref, cw_ref, cb_ref, w_ref,
                             x_ref, gate_ref, lng_ref, lnb_ref, o_ref, p_ref, y_ref,
                             *, alpha, per_seq, tc):
    tm, d = o_ref.shape
    first = (pl.program_id(0) % per_seq) == 0
    for c0 in range(0, d, tc):
        cs = slice(c0, c0 + tc)
        gated = cg_ref[:, cs].astype(F32) * u_ref[:, cs].astype(F32)
        halo = cgh_ref[:, cs].astype(F32) * uh_ref[:, cs].astype(F32)
        halo = jnp.where(first, 0.0, halo)
        g = jnp.concatenate([halo, gated], axis=0)
        g1 = pltpu.roll(g, 1, 0)[SUBLANES:]
        g2 = pltpu.roll(g, 2, 0)[SUBLANES:]
        conv = (cw_ref[0:1, cs] * g2 + cw_ref[1:2, cs] * g1 + cw_ref[2:3, cs] * gated
                + cb_ref[:, cs])
        p_ref[:, cs] = (bg_ref[:, cs].astype(F32) * conv).astype(BF16)
    y_ref[...] = jnp.dot(p_ref[...], w_ref[...], preferred_element_type=F32)
    _residual_ln_rows(y_ref, None, x_ref, gate_ref, lng_ref, lnb_ref, o_ref, alpha=alpha)


def _conv_residual_ln(bcu, conv_w, conv_b, w_out, layer, x, mod3, ln_g, ln_b, *, seq, alpha,
                      tm=512, tc=512):
    m, d = x.shape
    tm = min(tm, seq)
    tc = min(tc, d)
    per_seq = seq // tm
    hb = tm // SUBLANES
    row = lambda i: (0, 0)
    halo = lambda col: pl.BlockSpec((SUBLANES, d), lambda i: (jnp.maximum(i * hb - 1, 0), col))
    return pl.pallas_call(
        functools.partial(_conv_residual_ln_kernel, alpha=alpha, per_seq=per_seq, tc=tc),
        grid=(m // tm,),
        in_specs=[pl.BlockSpec((tm, d), lambda i: (i, 0)),
                  pl.BlockSpec((tm, d), lambda i: (i, 1)),
                  pl.BlockSpec((tm, d), lambda i: (i, 2)),
                  halo(1), halo(2),
                  pl.BlockSpec((CONV_WIDTH, d), row),
                  pl.BlockSpec((1, d), row),
                  pl.BlockSpec((None, d, d), lambda i: (layer, 0, 0)),
                  pl.BlockSpec((tm, d), lambda i: (i, 0)),
                  _mod_vec(d, per_seq, 2),
                  pl.BlockSpec((1, d), row),
                  pl.BlockSpec((1, d), row)],
        out_specs=pl.BlockSpec((tm, d), lambda i: (i, 0)),
        out_shape=jax.ShapeDtypeStruct((m, d), F32),
        scratch_shapes=[pltpu.VMEM((tm, d), BF16), pltpu.VMEM((tm, d), F32)],
        compiler_params=_params(("arbitrary",)),
        name="conv_residual_ln",
    )(bcu, bcu, bcu, bcu, bcu, conv_w, conv_b.reshape(1, d), w_out,
      x, mod3, ln_g.reshape(1, d), ln_b.reshape(1, d))


def _attention_kernel(q_ref, k_ref, v_ref, o_ref, z_ref, acc_ref, carry_ref, *, t, dh, scale):
    seq, width = q_ref.shape
    heads = width // dh
    row = lax.broadcasted_iota(jnp.int32, (t, t), 0)
    col = lax.broadcasted_iota(jnp.int32, (t, t), 1)
    causal = col < row
    later = jnp.where(row > col, 1.0, 0.0).astype(BF16)
    later2 = jnp.concatenate([later, later], axis=0)

    def rows(i):
        return pl.ds(pl.multiple_of(i * t, t), t)

    hds = [slice(h * dh, (h + 1) * dh) for h in range(heads)]

    def store_logits(qs, j, slot, h):
        z_ref[slot, h] = lax.dot_general(qs[h], k_ref[rows(j), hds[h]], (((1,), (1,)), ((), ())),
                                         preferred_element_type=F32)

    def key_block(qs, j, slot, diagonal):
        log_betas, l1ms, sticks = [], [], []
        for h in range(heads):
            z = z_ref[slot, h]
            neg_abs = lax.bitcast_convert_type(
                lax.bitcast_convert_type(z, jnp.uint32) | jnp.uint32(0x80000000), F32)
            log_beta = jnp.minimum(z, 0.0) - jnp.log(1.0 + jnp.exp2(neg_abs)) * LOG2_E
            l1m = log_beta - z
            if diagonal:
                l1m = jnp.where(causal, l1m, 0.0)
            hi = l1m.astype(BF16)
            lo = (l1m - hi.astype(F32)).astype(BF16)
            sticks.append(jnp.dot(jnp.concatenate([hi, lo], axis=1), later2,
                                  preferred_element_type=F32))
            log_betas.append(log_beta)
            l1ms.append(l1m)
        for h in range(heads):
            store_logits(qs, jnp.maximum(j - 1, 0), 1 - slot, h)
        for h, hd in enumerate(hds):
            carry = carry_ref[h]
            a = jnp.exp2(log_betas[h] + sticks[h] + carry)
            if diagonal:
                a = jnp.where(causal, a, 0.0)
            acc_ref[h] += jnp.dot(a.astype(BF16), v_ref[rows(j), hd], preferred_element_type=F32)
            carry_ref[h] = carry + (sticks[h][:, 0:1] + l1ms[h][:, 0:1])

    def q_tile(qi, _):
        qs = [(q_ref[rows(qi), hd].astype(F32) * (scale * LOG2_E)).astype(BF16) for hd in hds]
        acc_ref[...] = jnp.zeros_like(acc_ref)
        carry_ref[...] = jnp.zeros_like(carry_ref)
        for h in range(heads):
            store_logits(qs, qi, 0, h)
        key_block(qs, qi, 0, True)

        def below_diagonal(i, _):
            key_block(qs, qi - 1 - i, (i + 1) % 2, False)
            return 0

        lax.fori_loop(0, qi, below_diagonal, 0)
        for h, hd in enumerate(hds):
            o_ref[rows(qi), hd] = acc_ref[h].astype(o_ref.dtype)
        return 0

    lax.fori_loop(0, seq // t, q_tile, 0)


def _attention(qkv, *, bsz, seq, n_heads, t=256, heads_per_step=4):
    m, d3 = qkv.shape
    d = d3 // 3
    dh = d // n_heads
    t = min(t, seq)
    groups = n_heads // heads_per_step
    width = heads_per_step * dh
    return pl.pallas_call(
        functools.partial(_attention_kernel, t=t, dh=dh, scale=dh ** -0.5),
        grid=(bsz, groups),
        in_specs=[pl.BlockSpec((seq, width), lambda b, g: (b, g)),
                  pl.BlockSpec((seq, width), lambda b, g: (b, groups + g)),
                  pl.BlockSpec((seq, width), lambda b, g: (b, 2 * groups + g))],
        out_specs=pl.BlockSpec((seq, width), lambda b, g: (b, g)),
        out_shape=jax.ShapeDtypeStruct((m, d), BF16),
        scratch_shapes=[pltpu.VMEM((2, heads_per_step, t, t), F32),
                        pltpu.VMEM((heads_per_step, t, dh), F32),
                        pltpu.VMEM((heads_per_step, t, 1), F32)],
        compiler_params=_params(("arbitrary", "arbitrary")),
        name="stickbreak_attention",
    )(qkv, qkv, qkv)


def _trunk(x, c, mod_w, mod_b, ln_g, ln_b, attn_w_qkv, attn_w_o, conv_w_in, conv_w, conv_b,
           conv_w_out, mlp_w1, mlp_b1, mlp_w2, mlp_b2, *, n_heads):
    bsz, seq, d = x.shape
    depth = mod_w.shape[0]
    alpha = (2.0 * depth) ** 0.25
    xf = x.reshape(bsz * seq, d)
    mod = _modulation(c, mod_w, mod_b)
    w_qkv, w_o = _to_bf16(attn_w_qkv), _to_bf16(attn_w_o)
    w_in, w_out = _to_bf16(conv_w_in), _to_bf16(conv_w_out)
    w1, w2 = _to_bf16(mlp_w1), _to_bf16(mlp_w2)
    for i in range(depth):
        mod3 = mod[i].reshape(bsz * N_MOD, 1, d)
        j = i // 2
        if i % 2 == 0:
            qkv = _mod_matmul(xf, mod3, w_qkv, j, seq=seq)
            o = _attention(qkv, bsz=bsz, seq=seq, n_heads=n_heads)
            xf = _proj_residual_ln(o, w_o, j, xf, mod3, ln_g[i, 0], ln_b[i, 0],
                                   seq=seq, alpha=alpha)
        else:
            bcu = _mod_matmul(xf, mod3, w_in, j, seq=seq)
            xf = _conv_residual_ln(bcu, conv_w[j], conv_b[j], w_out, j, xf, mod3,
                                   ln_g[i, 0], ln_b[i, 0], seq=seq, alpha=alpha)
        xf = _mlp(xf, mod3, w1, mlp_b1[i], w2, mlp_b2[i], i, ln_g[i, 1], ln_b[i, 1],
                  seq=seq, alpha=alpha)
    return xf.reshape(bsz, seq, d)


def kernel(x, c, mod_w, mod_b, ln_g, ln_b, attn_w_qkv, attn_w_o, conv_w_in, conv_w, conv_b,
           conv_w_out, mlp_w1, mlp_b1, mlp_w2, mlp_b2):
    return _trunk(x, c, mod_w, mod_b, ln_g, ln_b, attn_w_qkv, attn_w_o, conv_w_in, conv_w, conv_b,
                  conv_w_out, mlp_w1, mlp_b1, mlp_w2, mlp_b2, n_heads=N_HEADS)
```

```python
import functools

import jax
import jax.numpy as jnp
from jax import lax
from jax.experimental import pallas as pl
from jax.experimental.pallas import tpu as pltpu

N_HEADS = 16
N_MOD = 6
CONV_WIDTH = 3
LN_EPS = 1e-5
LOG2_E = 1.4426950408889634
SUBLANES = 8
VMEM_LIMIT_BYTES = 56 * 1024 * 1024
CAST_BLOCK_ELEMS = 2 * 1024 * 1024

F32 = jnp.float32
BF16 = jnp.bfloat16


def _params(semantics):
    return pltpu.CompilerParams(dimension_semantics=semantics, vmem_limit_bytes=VMEM_LIMIT_BYTES)


def _cast_kernel(w_ref, o_ref):
    o_ref[...] = w_ref[...].astype(o_ref.dtype)


def _to_bf16(w):
    layers, k, n = w.shape
    tk = k
    while tk * n > CAST_BLOCK_ELEMS and tk % 32 == 0:
        tk //= 2
    return pl.pallas_call(
        _cast_kernel,
        grid=(layers, k // tk),
        in_specs=[pl.BlockSpec((1, tk, n), lambda l, i: (l, i, 0))],
        out_specs=pl.BlockSpec((1, tk, n), lambda l, i: (l, i, 0)),
        out_shape=jax.ShapeDtypeStruct(w.shape, BF16),
        compiler_params=_params(("arbitrary", "arbitrary")),
        name="to_bf16",
    )(w)


def _modulation_kernel(c_ref, w_ref, b_ref, o_ref):
    c = c_ref[...]
    cond = c * jax.nn.sigmoid(c)
    o_ref[0] = jnp.dot(cond, w_ref[0], preferred_element_type=F32,
                       precision=lax.Precision.HIGHEST) + b_ref[0]


def _modulation(c, mod_w, mod_b, *, tn=1024):
    depth, d, n = mod_w.shape
    bsz = c.shape[0]
    rows = -(-bsz // SUBLANES) * SUBLANES
    c_pad = jnp.zeros((rows, d), F32).at[:bsz].set(c)
    out = pl.pallas_call(
        _modulation_kernel,
        grid=(depth, n // tn),
        in_specs=[pl.BlockSpec((rows, d), lambda i, j: (0, 0)),
                  pl.BlockSpec((1, d, tn), lambda i, j: (i, 0, j)),
                  pl.BlockSpec((1, 1, tn), lambda i, j: (i, 0, j))],
        out_specs=pl.BlockSpec((1, rows, tn), lambda i, j: (i, 0, j)),
        out_shape=jax.ShapeDtypeStruct((depth, rows, n), F32),
        compiler_params=_params(("arbitrary", "arbitrary")),
        name="modulation",
    )(c_pad, mod_w, mod_b.reshape(depth, 1, n))
    return out[:, :bsz]


def _mod_vec(d, per_seq, j, tile=lambda i: i):
    return pl.BlockSpec((1, 1, d), lambda i, *_: ((tile(i) // per_seq) * N_MOD + j, 0, 0))


def _mod_matmul_kernel(x_ref, sc_ref, sh_ref, w_ref, o_ref, h_ref):
    @pl.when(pl.program_id(1) == 0)
    def _():
        h_ref[...] = (x_ref[...] * (1.0 + sc_ref[0]) + sh_ref[0]).astype(BF16)

    o_ref[...] = jnp.dot(h_ref[...], w_ref[...], preferred_element_type=F32).astype(o_ref.dtype)


def _mod_matmul(x, mod3, w, layer, *, seq, tm=1024, tn=1024):
    m, d = x.shape
    n = w.shape[2]
    tm = min(tm, seq)
    per_seq = seq // tm
    return pl.pallas_call(
        _mod_matmul_kernel,
        grid=(m // tm, n // tn),
        in_specs=[pl.BlockSpec((tm, d), lambda i, j: (i, 0)),
                  _mod_vec(d, per_seq, 1), _mod_vec(d, per_seq, 0),
                  pl.BlockSpec((None, d, tn), lambda i, j: (layer, 0, j))],
        out_specs=pl.BlockSpec((tm, tn), lambda i, j: (i, j)),
        out_shape=jax.ShapeDtypeStruct((m, n), BF16),
        scratch_shapes=[pltpu.VMEM((tm, d), BF16)],
        compiler_params=_params(("arbitrary", "arbitrary")),
        name="mod_matmul",
    )(x, mod3, mod3, w)


def _residual_ln(y, x, gate_ref, lng_ref, lnb_ref, *, alpha):
    r = alpha * x + (1.0 + gate_ref[0]) * y
    mu = jnp.mean(r, axis=-1, keepdims=True)
    rc = r - mu
    var = jnp.mean(rc * rc, axis=-1, keepdims=True)
    return rc * lax.rsqrt(var + LN_EPS) * lng_ref[...] + lnb_ref[...]


def _tile_step(n_tiles, start, product, bias_ref, acc_ref, xp_ref, gate_ref, lng_ref, lnb_ref, o_ref,
               *, alpha):
    i = pl.program_id(0)
    s = pl.program_id(1)
    cur = i % 2
    rows = o_ref.shape[0]

    def finish_previous_rows():
        y = acc_ref[1 - cur, pl.ds(pl.multiple_of(s * rows, rows), rows), :]
        if bias_ref is not None:
            y = y + bias_ref[...]
        o_ref[...] = _residual_ln(y, xp_ref[...], gate_ref, lng_ref, lnb_ref, alpha=alpha)

    if start is not None:
        pl.when((i < n_tiles) & (s == 0))(start)

    @pl.when((s == 0) & (i == 0))
    def _():
        acc_ref[1] = jnp.zeros(acc_ref.shape[1:], F32)

    @pl.when((i < n_tiles) & (s == 0))
    def _():
        finish_previous_rows()
        acc_ref[cur] = product()

    @pl.when((i < n_tiles) & (s > 0))
    def _():
        finish_previous_rows()
        acc_ref[cur] += product()

    @pl.when(i == n_tiles)
    def _():
        finish_previous_rows()


def _tile_grid(n_tiles, steps):
    this = lambda i: jnp.minimum(i, n_tiles - 1)
    prev = lambda i: jnp.maximum(i - 1, 0)
    step = lambda i, s: jnp.where(i < n_tiles, s, steps - 1)
    done = lambda i, s: jnp.where(i == 0, 0, prev(i) * steps + s)
    return this, prev, step, done


def _proj_residual_ln_kernel(a_ref, w_ref, xp_ref, gate_ref, lng_ref, lnb_ref, o_ref, acc_ref,
                             *, alpha, n_tiles):
    product = lambda: jnp.dot(a_ref[...], w_ref[...], preferred_element_type=F32)
    _tile_step(n_tiles, None, product, None, acc_ref, xp_ref, gate_ref, lng_ref, lnb_ref, o_ref,
               alpha=alpha)


def _proj_residual_ln(a, w, layer, x, mod3, ln_g, ln_b, *, seq, alpha, tm=512, steps=2):
    m, d = x.shape
    tm = min(tm, seq)
    per_seq = seq // tm
    n_tiles = m // tm
    tk, rows = d // steps, tm // steps
    this, prev, step, done = _tile_grid(n_tiles, steps)
    const = pl.BlockSpec((1, d), lambda i, s: (0, 0))
    return pl.pallas_call(
        functools.partial(_proj_residual_ln_kernel, alpha=alpha, n_tiles=n_tiles),
        grid=(n_tiles + 1, steps),
        in_specs=[pl.BlockSpec((tm, tk), lambda i, s: (this(i), step(i, s))),
                  pl.BlockSpec((None, tk, d), lambda i, s: (layer, step(i, s), 0)),
                  pl.BlockSpec((rows, d), lambda i, s: (done(i, s), 0)),
                  _mod_vec(d, per_seq, 2, prev), const, const],
        out_specs=pl.BlockSpec((rows, d), lambda i, s: (done(i, s), 0)),
        out_shape=jax.ShapeDtypeStruct((m, d), F32),
        scratch_shapes=[pltpu.VMEM((2, tm, d), F32)],
        compiler_params=_params(("arbitrary", "arbitrary")),
        name="proj_residual_ln",
    )(a, w, x, mod3, ln_g.reshape(1, d), ln_b.reshape(1, d))


def _mlp_kernel(x_ref, sc_ref, sh_ref, w1_ref, b1_ref, w2_ref, b2_ref, xp_ref, gate_ref, lng_ref,
                lnb_ref, o_ref, h_ref, acc_ref, *, alpha, n_tiles):
    def start():
        h_ref[...] = (x_ref[...] * (1.0 + sc_ref[0]) + sh_ref[0]).astype(BF16)

    def product():
        hid = jnp.dot(h_ref[...], w1_ref[...], preferred_element_type=F32) + b1_ref[...]
        hid = jnp.square(jnp.maximum(hid, 0.0)).astype(BF16)
        return jnp.dot(hid, w2_ref[...], preferred_element_type=F32)

    _tile_step(n_tiles, start, product, b2_ref, acc_ref, xp_ref, gate_ref, lng_ref, lnb_ref, o_ref,
               alpha=alpha)


def _mlp(x, mod3, w1, b1, w2, b2, layer, ln_g, ln_b, *, seq, alpha, tm=512, tf=1024):
    m, d = x.shape
    dff = w1.shape[2]
    tm = min(tm, seq)
    tf = min(tf, dff)
    per_seq = seq // tm
    n_tiles = m // tm
    steps = dff // tf
    rows = tm // steps
    this, prev, step, done = _tile_grid(n_tiles, steps)
    const = pl.BlockSpec((1, d), lambda i, s: (0, 0))
    return pl.pallas_call(
        functools.partial(_mlp_kernel, alpha=alpha, n_tiles=n_tiles),
        grid=(n_tiles + 1, steps),
        in_specs=[pl.BlockSpec((tm, d), lambda i, s: (this(i), 0)),
                  _mod_vec(d, per_seq, 4, this), _mod_vec(d, per_seq, 3, this),
                  pl.BlockSpec((None, d, tf), lambda i, s: (layer, 0, step(i, s))),
                  pl.BlockSpec((1, tf), lambda i, s: (0, step(i, s))),
                  pl.BlockSpec((None, tf, d), lambda i, s: (layer, step(i, s), 0)),
                  const,
                  pl.BlockSpec((rows, d), lambda i, s: (done(i, s), 0)),
                  _mod_vec(d, per_seq, 5, prev), const, const],
        out_specs=pl.BlockSpec((rows, d), lambda i, s: (done(i, s), 0)),
        out_shape=jax.ShapeDtypeStruct((m, d), F32),
        scratch_shapes=[pltpu.VMEM((tm, d), BF16), pltpu.VMEM((2, tm, d), F32)],
        compiler_params=_params(("arbitrary", "arbitrary")),
        name="mlp",
    )(x, mod3, mod3, w1, b1.reshape(1, dff), w2, b2.reshape(1, d), x, mod3,
      ln_g.reshape(1, d), ln_b.reshape(1, d))


def _conv_residual_ln_kernel(bg_ref, cg_ref, u_ref, cgh_ref, uh_ref, cw_ref, cb_ref, w_ref,
                             xp_ref, gate_ref, lng_ref, lnb_ref, o_ref, p_ref, acc_ref,
                             *, alpha, n_tiles, per_seq, tc):
    steps, _, tk = p_ref.shape
    first = (pl.program_id(0) % per_seq) == 0
    s = pl.program_id(1)

    def start():
        for c0 in range(0, steps * tk, tc):
            cs = slice(c0, c0 + tc)
            gated = cg_ref[:, cs].astype(F32) * u_ref[:, cs].astype(F32)
            halo = cgh_ref[:, cs].astype(F32) * uh_ref[:, cs].astype(F32)
            halo = jnp.where(first, 0.0, halo)
            g = jnp.concatenate([halo, gated], axis=0)
            g1 = pltpu.roll(g, 1, 0)[SUBLANES:]
            g2 = pltpu.roll(g, 2, 0)[SUBLANES:]
            conv = (cw_ref[0:1, cs] * g2 + cw_ref[1:2, cs] * g1 + cw_ref[2:3, cs] * gated
                    + cb_ref[:, cs])
            p_ref[c0 // tk, :, c0 % tk:c0 % tk + tc] = (bg_ref[:, cs].astype(F32) * conv).astype(BF16)

    product = lambda: jnp.dot(p_ref[s], w_ref[...], preferred_element_type=F32)
    _tile_step(n_tiles, start, product, None, acc_ref, xp_ref, gate_ref, lng_ref, lnb_ref, o_ref,
               alpha=alpha)


def _conv_residual_ln(bcu, conv_w, conv_b, w_out, layer, x, mod3, ln_g, ln_b, *, seq, alpha,
                      tm=512, steps=2, tc=256):
    m, d = x.shape
    tm = min(tm, seq)
    per_seq = seq // tm
    n_tiles = m // tm
    tk, rows = d // steps, tm // steps
    this, prev, step, done = _tile_grid(n_tiles, steps)
    hb = tm // SUBLANES
    const = lambda r: pl.BlockSpec((r, d), lambda i, s: (0, 0))
    tile = lambda col: pl.BlockSpec((tm, d), lambda i, s: (this(i), col))
    halo = lambda col: pl.BlockSpec((SUBLANES, d),
                                    lambda i, s: (jnp.maximum(this(i) * hb - 1, 0), col))
    return pl.pallas_call(
        functools.partial(_conv_residual_ln_kernel, alpha=alpha, n_tiles=n_tiles, per_seq=per_seq,
                          tc=min(tc, tk)),
        grid=(n_tiles + 1, steps),
        in_specs=[tile(0), tile(1), tile(2), halo(1), halo(2),
                  const(CONV_WIDTH), const(1),
                  pl.BlockSpec((None, tk, d), lambda i, s: (layer, step(i, s), 0)),
                  pl.BlockSpec((rows, d), lambda i, s: (done(i, s), 0)),
                  _mod_vec(d, per_seq, 2, prev), const(1), const(1)],
        out_specs=pl.BlockSpec((rows, d), lambda i, s: (done(i, s), 0)),
        out_shape=jax.ShapeDtypeStruct((m, d), F32),
        scratch_shapes=[pltpu.VMEM((steps, tm, tk), BF16), pltpu.VMEM((2, tm, d), F32)],
        compiler_params=_params(("arbitrary", "arbitrary")),
        name="conv_residual_ln",
    )(bcu, bcu, bcu, bcu, bcu, conv_w, conv_b.reshape(1, d), w_out,
      x, mod3, ln_g.reshape(1, d), ln_b.reshape(1, d))


def _attention_kernel(q_ref, k_ref, v_ref, o_ref, z_ref, acc_ref, carry_ref, *, t, dh, scale):
    seq, width = q_ref.shape
    heads = width // dh
    row = lax.broadcasted_iota(jnp.int32, (t, t), 0)
    col = lax.broadcasted_iota(jnp.int32, (t, t), 1)
    causal = col < row
    later = jnp.where(row > col, 1.0, 0.0).astype(BF16)
    later2 = jnp.concatenate([later, later], axis=0)

    def rows(i):
        return pl.ds(pl.multiple_of(i * t, t), t)

    hds = [slice(h * dh, (h + 1) * dh) for h in range(heads)]

    def store_logits(qs, j, slot, h):
        z_ref[slot, h] = lax.dot_general(qs[h], k_ref[rows(j), hds[h]], (((1,), (1,)), ((), ())),
                                         preferred_element_type=F32)

    def key_block(qs, j, slot, diagonal):
        log_betas, l1ms, sticks = [], [], []
        for h in range(heads):
            z = z_ref[slot, h]
            neg_abs = lax.bitcast_convert_type(
                lax.bitcast_convert_type(z, jnp.uint32) | jnp.uint32(0x80000000), F32)
            log_beta = jnp.minimum(z, 0.0) - jnp.log(1.0 + jnp.exp2(neg_abs)) * LOG2_E
            l1m = log_beta - z
            if diagonal:
                l1m = jnp.where(causal, l1m, 0.0)
            hi = l1m.astype(BF16)
            lo = (l1m - hi.astype(F32)).astype(BF16)
            sticks.append(jnp.dot(jnp.concatenate([hi, lo], axis=1), later2,
                                  preferred_element_type=F32))
            log_betas.append(log_beta)
            l1ms.append(l1m)
        for h in range(heads):
            store_logits(qs, jnp.maximum(j - 1, 0), 1 - slot, h)
        for h, hd in enumerate(hds):
            carry = carry_ref[h]
            a = jnp.exp2(log_betas[h] + sticks[h] + carry)
            if diagonal:
                a = jnp.where(causal, a, 0.0)
            acc_ref[h] += jnp.dot(a.astype(BF16), v_ref[rows(j), hd], preferred_element_type=F32)
            carry_ref[h] = carry + (sticks[h][:, 0:1] + l1ms[h][:, 0:1])

    def q_tile(qi, _):
        qs = [(q_ref[rows(qi), hd].astype(F32) * (scale * LOG2_E)).astype(BF16) for hd in hds]
        acc_ref[...] = jnp.zeros_like(acc_ref)
        carry_ref[...] = jnp.zeros_like(carry_ref)
        for h in range(heads):
            store_logits(qs, qi, 0, h)
        key_block(qs, qi, 0, True)

        def below_diagonal(i, _):
            key_block(qs, qi - 1 - i, (i + 1) % 2, False)
            return 0

        lax.fori_loop(0, qi, below_diagonal, 0)
        for h, hd in enumerate(hds):
            o_ref[rows(qi), hd] = acc_ref[h].astype(o_ref.dtype)
        return 0

    lax.fori_loop(0, seq // t, q_tile, 0)


def _attention(qkv, *, bsz, seq, n_heads, t=256, heads_per_step=4):
    m, d3 = qkv.shape
    d = d3 // 3
    dh = d // n_heads
    t = min(t, seq)
    groups = n_heads // heads_per_step
    width = heads_per_step * dh
    return pl.pallas_call(
        functools.partial(_attention_kernel, t=t, dh=dh, scale=dh ** -0.5),
        grid=(bsz, groups),
        in_specs=[pl.BlockSpec((seq, width), lambda b, g: (b, g)),
                  pl.BlockSpec((seq, width), lambda b, g: (b, groups + g)),
                  pl.BlockSpec((seq, width), lambda b, g: (b, 2 * groups + g))],
        out_specs=pl.BlockSpec((seq, width), lambda b, g: (b, g)),
        out_shape=jax.ShapeDtypeStruct((m, d), BF16),
        scratch_shapes=[pltpu.VMEM((2, heads_per_step, t, t), F32),
                        pltpu.VMEM((heads_per_step, t, dh), F32),
                        pltpu.VMEM((heads_per_step, t, 1), F32)],
        compiler_params=_params(("arbitrary", "arbitrary")),
        name="stickbreak_attention",
    )(qkv, qkv, qkv)


def _trunk(x, c, mod_w, mod_b, ln_g, ln_b, attn_w_qkv, attn_w_o, conv_w_in, conv_w, conv_b,
           conv_w_out, mlp_w1, mlp_b1, mlp_w2, mlp_b2, *, n_heads):
    bsz, seq, d = x.shape
    depth = mod_w.shape[0]
    alpha = (2.0 * depth) ** 0.25
    xf = x.reshape(bsz * seq, d)
    mod = _modulation(c, mod_w, mod_b)
    w_qkv, w_o = _to_bf16(attn_w_qkv), _to_bf16(attn_w_o)
    w_in, w_out = _to_bf16(conv_w_in), _to_bf16(conv_w_out)
    w1, w2 = _to_bf16(mlp_w1), _to_bf16(mlp_w2)
    for i in range(depth):
        mod3 = mod[i].reshape(bsz * N_MOD, 1, d)
        j = i // 2
        if i % 2 == 0:
            qkv = _mod_matmul(xf, mod3, w_qkv, j, seq=seq)
            o = _attention(qkv, bsz=bsz, seq=seq, n_heads=n_heads)
            xf = _proj_residual_ln(o, w_o, j, xf, mod3, ln_g[i, 0], ln_b[i, 0],
                                   seq=seq, alpha=alpha)
        else:
            bcu = _mod_matmul(xf, mod3, w_in, j, seq=seq)
            xf = _conv_residual_ln(bcu, conv_w[j], conv_b[j], w_out, j, xf, mod3,
                                   ln_g[i, 0], ln_b[i, 0], seq=seq, alpha=alpha)
        xf = _mlp(xf, mod3, w1, mlp_b1[i], w2, mlp_b2[i], i, ln_g[i, 1], ln_b[i, 1],
                  seq=seq, alpha=alpha)
    return xf.reshape(bsz, seq, d)


def kernel(x, c, mod_w, mod_b, ln_g, ln_b, attn_w_qkv, attn_w_o, conv_w_in, conv_w, conv_b,
           conv_w_out, mlp_w1, mlp_b1, mlp_w2, mlp_b2):
    return _trunk(x, c, mod_w, mod_b, ln_g, ln_b, attn_w_qkv, attn_w_o, conv_w_in, conv_w, conv_b,
                  conv_w_out, mlp_w1, mlp_b1, mlp_w2, mlp_b2, n_heads=N_HEADS)
```

```python
import functools

import jax
import jax.numpy as jnp
from jax import lax
from jax.experimental import pallas as pl
from jax.experimental.pallas import tpu as pltpu

N_HEADS = 16
N_MOD = 6
CONV_WIDTH = 3
LN_EPS = 1e-5
LOG2_E = 1.4426950408889634
SUBLANES = 8
VMEM_LIMIT_BYTES = 56 * 1024 * 1024
CAST_BLOCK_ELEMS = 2 * 1024 * 1024

F32 = jnp.float32
BF16 = jnp.bfloat16


def _params(semantics):
    return pltpu.CompilerParams(dimension_semantics=semantics, vmem_limit_bytes=VMEM_LIMIT_BYTES)


def _cast_kernel(w_ref, o_ref):
    o_ref[...] = w_ref[...].astype(o_ref.dtype)


def _to_bf16(w):
    layers, k, n = w.shape
    tk = k
    while tk * n > CAST_BLOCK_ELEMS and tk % 32 == 0:
        tk //= 2
    return pl.pallas_call(
        _cast_kernel,
        grid=(layers, k // tk),
        in_specs=[pl.BlockSpec((1, tk, n), lambda l, i: (l, i, 0))],
        out_specs=pl.BlockSpec((1, tk, n), lambda l, i: (l, i, 0)),
        out_shape=jax.ShapeDtypeStruct(w.shape, BF16),
        compiler_params=_params(("arbitrary", "arbitrary")),
        name="to_bf16",
    )(w)


def _modulation_kernel(c_ref, w_ref, b_ref, o_ref):
    c = c_ref[...]
    cond = c * jax.nn.sigmoid(c)
    o_ref[0] = jnp.dot(cond, w_ref[0], preferred_element_type=F32,
                       precision=lax.Precision.HIGHEST) + b_ref[0]


def _modulation(c, mod_w, mod_b, *, tn=1024):
    depth, d, n = mod_w.shape
    bsz = c.shape[0]
    rows = -(-bsz // SUBLANES) * SUBLANES
    c_pad = jnp.zeros((rows, d), F32).at[:bsz].set(c)
    out = pl.pallas_call(
        _modulation_kernel,
        grid=(depth, n // tn),
        in_specs=[pl.BlockSpec((rows, d), lambda i, j: (0, 0)),
                  pl.BlockSpec((1, d, tn), lambda i, j: (i, 0, j)),
                  pl.BlockSpec((1, 1, tn), lambda i, j: (i, 0, j))],
        out_specs=pl.BlockSpec((1, rows, tn), lambda i, j: (i, 0, j)),
        out_shape=jax.ShapeDtypeStruct((depth, rows, n), F32),
        compiler_params=_params(("arbitrary", "arbitrary")),
        name="modulation",
    )(c_pad, mod_w, mod_b.reshape(depth, 1, n))
    return out[:, :bsz]


def _mod_vec(d, per_seq, j, tile=lambda i: i):
    return pl.BlockSpec((1, 1, d), lambda i, *_: ((tile(i) // per_seq) * N_MOD + j, 0, 0))


def _mod_matmul_kernel(x_ref, sc_ref, sh_ref, w_ref, o_ref, h_ref):
    @pl.when(pl.program_id(1) == 0)
    def _():
        h_ref[...] = (x_ref[...] * (1.0 + sc_ref[0]) + sh_ref[0]).astype(BF16)

    o_ref[...] = jnp.dot(h_ref[...], w_ref[...], preferred_element_type=F32).astype(o_ref.dtype)


def _mod_matmul(x, mod3, w, layer, *, seq, tm=1024, tn=1024):
    m, d = x.shape
    n = w.shape[2]
    tm = min(tm, seq)
    per_seq = seq // tm
    return pl.pallas_call(
        _mod_matmul_kernel,
        grid=(m // tm, n // tn),
        in_specs=[pl.BlockSpec((tm, d), lambda i, j: (i, 0)),
                  _mod_vec(d, per_seq, 1), _mod_vec(d, per_seq, 0),
                  pl.BlockSpec((None, d, tn), lambda i, j: (layer, 0, j))],
        out_specs=pl.BlockSpec((tm, tn), lambda i, j: (i, j)),
        out_shape=jax.ShapeDtypeStruct((m, n), BF16),
        scratch_shapes=[pltpu.VMEM((tm, d), BF16)],
        compiler_params=_params(("arbitrary", "arbitrary")),
        name="mod_matmul",
    )(x, mod3, mod3, w)


def _residual_ln(y, x, gate_ref, lng_ref, lnb_ref, *, alpha):
    r = alpha * x + (1.0 + gate_ref[0]) * y
    mu = jnp.mean(r, axis=-1, keepdims=True)
    rc = r - mu
    var = jnp.mean(rc * rc, axis=-1, keepdims=True)
    return rc * lax.rsqrt(var + LN_EPS) * lng_ref[...] + lnb_ref[...]


def _tile_step(n_tiles, start, product, bias_ref, acc_ref, xp_ref, gate_ref, lng_ref, lnb_ref, o_ref,
               *, alpha):
    i = pl.program_id(0)
    s = pl.program_id(1)
    cur = i % 2
    rows = o_ref.shape[0]

    def finish_previous_rows():
        y = acc_ref[1 - cur, pl.ds(pl.multiple_of(s * rows, rows), rows), :]
        if bias_ref is not None:
            y = y + bias_ref[...]
        o_ref[...] = _residual_ln(y, xp_ref[...], gate_ref, lng_ref, lnb_ref, alpha=alpha)

    if start is not None:
        pl.when((i < n_tiles) & (s == 0))(start)

    @pl.when((s == 0) & (i == 0))
    def _():
        acc_ref[1] = jnp.zeros(acc_ref.shape[1:], F32)

    @pl.when((i < n_tiles) & (s == 0))
    def _():
        finish_previous_rows()
        acc_ref[cur] = product()

    @pl.when((i < n_tiles) & (s > 0))
    def _():
        finish_previous_rows()
        acc_ref[cur] += product()

    @pl.when(i == n_tiles)
    def _():
        finish_previous_rows()


def _tile_grid(n_tiles, steps):
    this = lambda i: jnp.minimum(i, n_tiles - 1)
    prev = lambda i: jnp.maximum(i - 1, 0)
    step = lambda i, s: jnp.where(i < n_tiles, s, steps - 1)
    done = lambda i, s: jnp.where(i == 0, 0, prev(i) * steps + s)
    return this, prev, step, done


def _proj_residual_ln_kernel(a_ref, w_ref, xp_ref, gate_ref, lng_ref, lnb_ref, o_ref, acc_ref,
                             *, alpha, n_tiles):
    tk = a_ref.shape[1]
    k0 = pl.multiple_of(pl.program_id(1) * tk, tk)
    product = lambda: jnp.dot(a_ref[...], w_ref[pl.ds(k0, tk), :], preferred_element_type=F32)
    _tile_step(n_tiles, None, product, None, acc_ref, xp_ref, gate_ref, lng_ref, lnb_ref, o_ref,
               alpha=alpha)


def _proj_residual_ln(a, w, layer, x, mod3, ln_g, ln_b, *, seq, alpha, tm=512, steps=2):
    m, d = x.shape
    tm = min(tm, seq)
    per_seq = seq // tm
    n_tiles = m // tm
    tk, rows = d // steps, tm // steps
    this, prev, step, done = _tile_grid(n_tiles, steps)
    const = pl.BlockSpec((1, d), lambda i, s: (0, 0))
    return pl.pallas_call(
        functools.partial(_proj_residual_ln_kernel, alpha=alpha, n_tiles=n_tiles),
        grid=(n_tiles + 1, steps),
        in_specs=[pl.BlockSpec((tm, tk), lambda i, s: (this(i), step(i, s))),
                  pl.BlockSpec((None, d, d), lambda i, s: (layer, 0, 0)),
                  pl.BlockSpec((rows, d), lambda i, s: (done(i, s), 0)),
                  _mod_vec(d, per_seq, 2, prev), const, const],
        out_specs=pl.BlockSpec((rows, d), lambda i, s: (done(i, s), 0)),
        out_shape=jax.ShapeDtypeStruct((m, d), F32),
        scratch_shapes=[pltpu.VMEM((2, tm, d), F32)],
        compiler_params=_params(("arbitrary", "arbitrary")),
        name="proj_residual_ln",
    )(a, w, x, mod3, ln_g.reshape(1, d), ln_b.reshape(1, d))


def _mlp_kernel(x_ref, sc_ref, sh_ref, w1_ref, b1_ref, w2_ref, b2_ref, xp_ref, gate_ref, lng_ref,
                lnb_ref, o_ref, h_ref, acc_ref, *, alpha, n_tiles):
    def start():
        h_ref[...] = (x_ref[...] * (1.0 + sc_ref[0]) + sh_ref[0]).astype(BF16)

    def product():
        hid = jnp.dot(h_ref[...], w1_ref[...], preferred_element_type=F32) + b1_ref[...]
        hid = jnp.square(jnp.maximum(hid, 0.0)).astype(BF16)
        return jnp.dot(hid, w2_ref[...], preferred_element_type=F32)

    _tile_step(n_tiles, start, product, b2_ref, acc_ref, xp_ref, gate_ref, lng_ref, lnb_ref, o_ref,
               alpha=alpha)


def _mlp(x, mod3, w1, b1, w2, b2, layer, ln_g, ln_b, *, seq, alpha, tm=1024, tf=512):
    m, d = x.shape
    dff = w1.shape[2]
    tm = min(tm, seq)
    tf = min(tf, dff)
    per_seq = seq // tm
    n_tiles = m // tm
    steps = dff // tf
    rows = tm // steps
    this, prev, step, done = _tile_grid(n_tiles, steps)
    const = pl.BlockSpec((1, d), lambda i, s: (0, 0))
    return pl.pallas_call(
        functools.partial(_mlp_kernel, alpha=alpha, n_tiles=n_tiles),
        grid=(n_tiles + 1, steps),
        in_specs=[pl.BlockSpec((tm, d), lambda i, s: (this(i), 0)),
                  _mod_vec(d, per_seq, 4, this), _mod_vec(d, per_seq, 3, this),
                  pl.BlockSpec((None, d, tf), lambda i, s: (layer, 0, step(i, s))),
                  pl.BlockSpec((1, tf), lambda i, s: (0, step(i, s))),
                  pl.BlockSpec((None, tf, d), lambda i, s: (layer, step(i, s), 0)),
                  const,
                  pl.BlockSpec((rows, d), lambda i, s: (done(i, s), 0)),
                  _mod_vec(d, per_seq, 5, prev), const, const],
        out_specs=pl.BlockSpec((rows, d), lambda i, s: (done(i, s), 0)),
        out_shape=jax.ShapeDtypeStruct((m, d), F32),
        scratch_shapes=[pltpu.VMEM((tm, d), BF16), pltpu.VMEM((2, tm, d), F32)],
        compiler_params=_params(("arbitrary", "arbitrary")),
        name="mlp",
    )(x, mod3, mod3, w1, b1.reshape(1, dff), w2, b2.reshape(1, d), x, mod3,
      ln_g.reshape(1, d), ln_b.reshape(1, d))


def _conv_residual_ln_kernel(bg_ref, cg_ref, u_ref, cgh_ref, uh_ref, cw_ref, cb_ref, w_ref,
                             xp_ref, gate_ref, lng_ref, lnb_ref, o_ref, p_ref, acc_ref,
                             *, alpha, n_tiles, per_seq, tc):
    steps, _, tk = p_ref.shape
    first = (pl.program_id(0) % per_seq) == 0
    s = pl.program_id(1)

    def start():
        for c0 in range(0, steps * tk, tc):
            cs = slice(c0, c0 + tc)
            gated = cg_ref[:, cs].astype(F32) * u_ref[:, cs].astype(F32)
            halo = cgh_ref[:, cs].astype(F32) * uh_ref[:, cs].astype(F32)
            halo = jnp.where(first, 0.0, halo)
            g = jnp.concatenate([halo, gated], axis=0)
            g1 = pltpu.roll(g, 1, 0)[SUBLANES:]
            g2 = pltpu.roll(g, 2, 0)[SUBLANES:]
            conv = (cw_ref[0:1, cs] * g2 + cw_ref[1:2, cs] * g1 + cw_ref[2:3, cs] * gated
                    + cb_ref[:, cs])
            p_ref[c0 // tk, :, c0 % tk:c0 % tk + tc] = (bg_ref[:, cs].astype(F32) * conv).astype(BF16)

    k0 = pl.multiple_of(s * tk, tk)
    product = lambda: jnp.dot(p_ref[s], w_ref[pl.ds(k0, tk), :], preferred_element_type=F32)
    _tile_step(n_tiles, start, product, None, acc_ref, xp_ref, gate_ref, lng_ref, lnb_ref, o_ref,
               alpha=alpha)


def _conv_residual_ln(bcu, conv_w, conv_b, w_out, layer, x, mod3, ln_g, ln_b, *, seq, alpha,
                      tm=512, steps=2, tc=256):
    m, d = x.shape
    tm = min(tm, seq)
    per_seq = seq // tm
    n_tiles = m // tm
    tk, rows = d // steps, tm // steps
    this, prev, step, done = _tile_grid(n_tiles, steps)
    hb = tm // SUBLANES
    const = lambda r: pl.BlockSpec((r, d), lambda i, s: (0, 0))
    tile = lambda col: pl.BlockSpec((tm, d), lambda i, s: (this(i), col))
    halo = lambda col: pl.BlockSpec((SUBLANES, d),
                                    lambda i, s: (jnp.maximum(this(i) * hb - 1, 0), col))
    return pl.pallas_call(
        functools.partial(_conv_residual_ln_kernel, alpha=alpha, n_tiles=n_tiles, per_seq=per_seq,
                          tc=min(tc, tk)),
        grid=(n_tiles + 1, steps),
        in_specs=[tile(0), tile(1), tile(2), halo(1), halo(2),
                  const(CONV_WIDTH), const(1),
                  pl.BlockSpec((None, d, d), lambda i, s: (layer, 0, 0)),
                  pl.BlockSpec((rows, d), lambda i, s: (done(i, s), 0)),
                  _mod_vec(d, per_seq, 2, prev), const(1), const(1)],
        out_specs=pl.BlockSpec((rows, d), lambda i, s: (done(i, s), 0)),
        out_shape=jax.ShapeDtypeStruct((m, d), F32),
        scratch_shapes=[pltpu.VMEM((steps, tm, tk), BF16), pltpu.VMEM((2, tm, d), F32)],
        compiler_params=_params(("arbitrary", "arbitrary")),
        name="conv_residual_ln",
    )(bcu, bcu, bcu, bcu, bcu, conv_w, conv_b.reshape(1, d), w_out,
      x, mod3, ln_g.reshape(1, d), ln_b.reshape(1, d))


def _attention_kernel(q_ref, k_ref, v_ref, o_ref, z_ref, acc_ref, carry_ref, *, t, dh, scale):
    seq, width = q_ref.shape
    heads = width // dh
    row = lax.broadcasted_iota(jnp.int32, (t, t), 0)
    col = lax.broadcasted_iota(jnp.int32, (t, t), 1)
    causal = col < row
    later = jnp.where(row > col, 1.0, 0.0).astype(BF16)
    later2 = jnp.concatenate([later, later], axis=0)

    def rows(i):
        return pl.ds(pl.multiple_of(i * t, t), t)

    hds = [slice(h * dh, (h + 1) * dh) for h in range(heads)]

    def store_logits(qs, j, h):
        z_ref[h] = lax.dot_general(qs[h], k_ref[rows(j), hds[h]], (((1,), (1,)), ((), ())),
                                   preferred_element_type=F32)

    def key_block(qs, j, diagonal):
        log_betas, l1ms, sticks = [], [], []
        for h in range(heads):
            z = z_ref[h]
            neg_abs = lax.bitcast_convert_type(
                lax.bitcast_convert_type(z, jnp.uint32) | jnp.uint32(0x80000000), F32)
            log_beta = jnp.minimum(z, 0.0) - jnp.log(1.0 + jnp.exp2(neg_abs)) * LOG2_E
            l1m = log_beta - z
            if diagonal:
                l1m = jnp.where(causal, l1m, 0.0)
            hi = l1m.astype(BF16)
            lo = (l1m - hi.astype(F32)).astype(BF16)
            sticks.append(jnp.dot(jnp.concatenate([hi, lo], axis=1), later2,
                                  preferred_element_type=F32))
            log_betas.append(log_beta)
            l1ms.append(l1m)
        for h in range(heads):
            store_logits(qs, jnp.maximum(j - 1, 0), h)
        for h, hd in enumerate(hds):
            carry = carry_ref[h]
            a = jnp.exp2(log_betas[h] + sticks[h] + carry)
            if diagonal:
                a = jnp.where(causal, a, 0.0)
            acc_ref[h] += jnp.dot(a.astype(BF16), v_ref[rows(j), hd], preferred_element_type=F32)
            carry_ref[h] = carry + (sticks[h][:, 0:1] + l1ms[h][:, 0:1])

    def q_tile(qi, _):
        qs = [(q_ref[rows(qi), hd].astype(F32) * (scale * LOG2_E)).astype(BF16) for hd in hds]
        acc_ref[...] = jnp.zeros_like(acc_ref)
        carry_ref[...] = jnp.zeros_like(carry_ref)
        for h in range(heads):
            store_logits(qs, qi, h)
        key_block(qs, qi, True)

        def below_diagonal(i, _):
            key_block(qs, qi - 1 - i, False)
            return 0

        lax.fori_loop(0, qi, below_diagonal, 0)
        for h, hd in enumerate(hds):
            o_ref[rows(qi), hd] = acc_ref[h].astype(o_ref.dtype)
        return 0

    lax.fori_loop(0, seq // t, q_tile, 0)


def _attention(qkv, *, bsz, seq, n_heads, t=256, heads_per_step=4):
    m, d3 = qkv.shape
    d = d3 // 3
    dh = d // n_heads
    t = min(t, seq)
    groups = n_heads // heads_per_step
    width = heads_per_step * dh
    return pl.pallas_call(
        functools.partial(_attention_kernel, t=t, dh=dh, scale=dh ** -0.5),
        grid=(bsz, groups),
        in_specs=[pl.BlockSpec((seq, width), lambda b, g: (b, g)),
                  pl.BlockSpec((seq, width), lambda b, g: (b, groups + g)),
                  pl.BlockSpec((seq, width), lambda b, g: (b, 2 * groups + g))],
        out_specs=pl.BlockSpec((seq, width), lambda b, g: (b, g)),
        out_shape=jax.ShapeDtypeStruct((m, d), BF16),
        scratch_shapes=[pltpu.VMEM((heads_per_step, t, t), F32),
                        pltpu.VMEM((heads_per_step, t, dh), F32),
                        pltpu.VMEM((heads_per_step, t, 1), F32)],
        compiler_params=_params(("arbitrary", "arbitrary")),
        name="stickbreak_attention",
    )(qkv, qkv, qkv)


def _trunk(x, c, mod_w, mod_b, ln_g, ln_b, attn_w_qkv, attn_w_o, conv_w_in, conv_w, conv_b,
           conv_w_out, mlp_w1, mlp_b1, mlp_w2, mlp_b2, *, n_heads):
    bsz, seq, d = x.shape
    depth = mod_w.shape[0]
    alpha = (2.0 * depth) ** 0.25
    xf = x.reshape(bsz * seq, d)
    mod = _modulation(c, mod_w, mod_b)
    w_qkv, w_o = _to_bf16(attn_w_qkv), _to_bf16(attn_w_o)
    w_in, w_out = _to_bf16(conv_w_in), _to_bf16(conv_w_out)
    w1, w2 = _to_bf16(mlp_w1), _to_bf16(mlp_w2)
    for i in range(depth):
        mod3 = mod[i].reshape(bsz * N_MOD, 1, d)
        j = i // 2
        if i % 2 == 0:
            qkv = _mod_matmul(xf, mod3, w_qkv, j, seq=seq)
            o = _attention(qkv, bsz=bsz, seq=seq, n_heads=n_heads)
            xf = _proj_residual_ln(o, w_o, j, xf, mod3, ln_g[i, 0], ln_b[i, 0],
                                   seq=seq, alpha=alpha)
        else:
            bcu = _mod_matmul(xf, mod3, w_in, j, seq=seq)
            xf = _conv_residual_ln(bcu, conv_w[j], conv_b[j], w_out, j, xf, mod3,
                                   ln_g[i, 0], ln_b[i, 0], seq=seq, alpha=alpha)
        xf = _mlp(xf, mod3, w1, mlp_b1[i], w2, mlp_b2[i], i, ln_g[i, 1], ln_b[i, 1],
                  seq=seq, alpha=alpha)
    return xf.reshape(bsz, seq, d)


def kernel(x, c, mod_w, mod_b, ln_g, ln_b, attn_w_qkv, attn_w_o, conv_w_in, conv_w, conv_b,
           conv_w_out, mlp_w1, mlp_b1, mlp_w2, mlp_b2):
    return _trunk(x, c, mod_w, mod_b, ln_g, ln_b, attn_w_qkv, attn_w_o, conv_w_in, conv_w, conv_b,
                  conv_w_out, mlp_w1, mlp_b1, mlp_w2, mlp_b2, n_heads=N_HEADS)
```

```python
import functools

import jax
import jax.numpy as jnp
from jax import lax
from jax.experimental import pallas as pl
from jax.experimental.pallas import tpu as pltpu

N_HEADS = 16
N_MOD = 6
CONV_WIDTH = 3
LN_EPS = 1e-5
LOG2_E = 1.4426950408889634
SUBLANES = 8
VMEM_LIMIT_BYTES = 56 * 1024 * 1024
CAST_BLOCK_ELEMS = 2 * 1024 * 1024

F32 = jnp.float32
BF16 = jnp.bfloat16


def _params(semantics):
    return pltpu.CompilerParams(dimension_semantics=semantics, vmem_limit_bytes=VMEM_LIMIT_BYTES)


def _cast_kernel(w_ref, o_ref):
    o_ref[...] = w_ref[...].astype(o_ref.dtype)


def _to_bf16(w):
    layers, k, n = w.shape
    tk = k
    while tk * n > CAST_BLOCK_ELEMS and tk % 32 == 0:
        tk //= 2
    return pl.pallas_call(
        _cast_kernel,
        grid=(layers, k // tk),
        in_specs=[pl.BlockSpec((1, tk, n), lambda l, i: (l, i, 0))],
        out_specs=pl.BlockSpec((1, tk, n), lambda l, i: (l, i, 0)),
        out_shape=jax.ShapeDtypeStruct(w.shape, BF16),
        compiler_params=_params(("arbitrary", "arbitrary")),
        name="to_bf16",
    )(w)


def _modulation_kernel(c_ref, w_ref, b_ref, o_ref):
    c = c_ref[...]
    cond = c * jax.nn.sigmoid(c)
    o_ref[0] = jnp.dot(cond, w_ref[0], preferred_element_type=F32,
                       precision=lax.Precision.HIGHEST) + b_ref[0]


def _modulation(c, mod_w, mod_b, *, tn=1024):
    depth, d, n = mod_w.shape
    bsz = c.shape[0]
    rows = -(-bsz // SUBLANES) * SUBLANES
    c_pad = jnp.zeros((rows, d), F32).at[:bsz].set(c)
    out = pl.pallas_call(
        _modulation_kernel,
        grid=(depth, n // tn),
        in_specs=[pl.BlockSpec((rows, d), lambda i, j: (0, 0)),
                  pl.BlockSpec((1, d, tn), lambda i, j: (i, 0, j)),
                  pl.BlockSpec((1, 1, tn), lambda i, j: (i, 0, j))],
        out_specs=pl.BlockSpec((1, rows, tn), lambda i, j: (i, 0, j)),
        out_shape=jax.ShapeDtypeStruct((depth, rows, n), F32),
        compiler_params=_params(("arbitrary", "arbitrary")),
        name="modulation",
    )(c_pad, mod_w, mod_b.reshape(depth, 1, n))
    return out[:, :bsz]


def _mod_vec(d, per_seq, j, tile=lambda i: i):
    return pl.BlockSpec((1, 1, d), lambda i, *_: ((tile(i) // per_seq) * N_MOD + j, 0, 0))


def _mod_matmul_kernel(x_ref, sc_ref, sh_ref, w_ref, o_ref, h_ref):
    @pl.when(pl.program_id(1) == 0)
    def _():
        h_ref[...] = (x_ref[...] * (1.0 + sc_ref[0]) + sh_ref[0]).astype(BF16)

    o_ref[...] = jnp.dot(h_ref[...], w_ref[...], preferred_element_type=F32).astype(o_ref.dtype)


def _mod_matmul(x, mod3, w, layer, *, seq, tm=1024, tn=1024):
    m, d = x.shape
    n = w.shape[2]
    tm = min(tm, seq)
    per_seq = seq // tm
    return pl.pallas_call(
        _mod_matmul_kernel,
        grid=(m // tm, n // tn),
        in_specs=[pl.BlockSpec((tm, d), lambda i, j: (i, 0)),
                  _mod_vec(d, per_seq, 1), _mod_vec(d, per_seq, 0),
                  pl.BlockSpec((None, d, tn), lambda i, j: (layer, 0, j))],
        out_specs=pl.BlockSpec((tm, tn), lambda i, j: (i, j)),
        out_shape=jax.ShapeDtypeStruct((m, n), BF16),
        scratch_shapes=[pltpu.VMEM((tm, d), BF16)],
        compiler_params=_params(("arbitrary", "arbitrary")),
        name="mod_matmul",
    )(x, mod3, mod3, w)


def _residual_ln(y, x, gate_ref, lng_ref, lnb_ref, *, alpha):
    r = alpha * x + (1.0 + gate_ref[0]) * y
    mu = jnp.mean(r, axis=-1, keepdims=True)
    rc = r - mu
    var = jnp.mean(rc * rc, axis=-1, keepdims=True)
    return rc * lax.rsqrt(var + LN_EPS) * lng_ref[...] + lnb_ref[...]


def _tile_step(n_tiles, start, product, bias_ref, acc_ref, xp_ref, gate_ref, lng_ref, lnb_ref, o_ref,
               *, alpha):
    i = pl.program_id(0)
    s = pl.program_id(1)
    cur = i % 2
    rows = o_ref.shape[0]

    def finish_previous_rows():
        y = acc_ref[1 - cur, pl.ds(pl.multiple_of(s * rows, rows), rows), :]
        if bias_ref is not None:
            y = y + bias_ref[...]
        o_ref[...] = _residual_ln(y, xp_ref[...], gate_ref, lng_ref, lnb_ref, alpha=alpha)

    if start is not None:
        pl.when((i < n_tiles) & (s == 0))(start)

    @pl.when((s == 0) & (i == 0))
    def _():
        acc_ref[1] = jnp.zeros(acc_ref.shape[1:], F32)

    @pl.when((i < n_tiles) & (s == 0))
    def _():
        finish_previous_rows()
        acc_ref[cur] = product()

    @pl.when((i < n_tiles) & (s > 0))
    def _():
        finish_previous_rows()
        acc_ref[cur] += product()

    @pl.when(i == n_tiles)
    def _():
        finish_previous_rows()


def _tile_grid(n_tiles, steps):
    this = lambda i: jnp.minimum(i, n_tiles - 1)
    prev = lambda i: jnp.maximum(i - 1, 0)
    step = lambda i, s: jnp.where(i < n_tiles, s, steps - 1)
    done = lambda i, s: jnp.where(i == 0, 0, prev(i) * steps + s)
    return this, prev, step, done


def _proj_residual_ln_kernel(a_ref, w_ref, xp_ref, gate_ref, lng_ref, lnb_ref, o_ref, acc_ref,
                             *, alpha, n_tiles):
    tk = a_ref.shape[1]
    k0 = pl.multiple_of(pl.program_id(1) * tk, tk)
    product = lambda: jnp.dot(a_ref[...], w_ref[pl.ds(k0, tk), :], preferred_element_type=F32)
    _tile_step(n_tiles, None, product, None, acc_ref, xp_ref, gate_ref, lng_ref, lnb_ref, o_ref,
               alpha=alpha)


def _proj_residual_ln(a, w, layer, x, mod3, ln_g, ln_b, *, seq, alpha, tm=512, steps=2):
    m, d = x.shape
    tm = min(tm, seq)
    per_seq = seq // tm
    n_tiles = m // tm
    tk, rows = d // steps, tm // steps
    this, prev, step, done = _tile_grid(n_tiles, steps)
    const = pl.BlockSpec((1, d), lambda i, s: (0, 0))
    return pl.pallas_call(
        functools.partial(_proj_residual_ln_kernel, alpha=alpha, n_tiles=n_tiles),
        grid=(n_tiles + 1, steps),
        in_specs=[pl.BlockSpec((tm, tk), lambda i, s: (this(i), step(i, s))),
                  pl.BlockSpec((None, d, d), lambda i, s: (layer, 0, 0)),
                  pl.BlockSpec((rows, d), lambda i, s: (done(i, s), 0)),
                  _mod_vec(d, per_seq, 2, prev), const, const],
        out_specs=pl.BlockSpec((rows, d), lambda i, s: (done(i, s), 0)),
        out_shape=jax.ShapeDtypeStruct((m, d), F32),
        scratch_shapes=[pltpu.VMEM((2, tm, d), F32)],
        compiler_params=_params(("arbitrary", "arbitrary")),
        name="proj_residual_ln",
    )(a, w, x, mod3, ln_g.reshape(1, d), ln_b.reshape(1, d))


def _mlp_kernel(x_ref, sc_ref, sh_ref, w1_ref, b1_ref, w2_ref, b2_ref, xp_ref, gate_ref, lng_ref,
                lnb_ref, o_ref, h_ref, acc_ref, *, alpha, n_tiles):
    def start():
        h_ref[...] = (x_ref[...] * (1.0 + sc_ref[0]) + sh_ref[0]).astype(BF16)

    def product():
        hid = jnp.dot(h_ref[...], w1_ref[...], preferred_element_type=F32) + b1_ref[...]
        hid = jnp.square(jnp.maximum(hid, 0.0)).astype(BF16)
        return jnp.dot(hid, w2_ref[...], preferred_element_type=F32)

    _tile_step(n_tiles, start, product, b2_ref, acc_ref, xp_ref, gate_ref, lng_ref, lnb_ref, o_ref,
               alpha=alpha)


def _mlp(x, mod3, w1, b1, w2, b2, layer, ln_g, ln_b, *, seq, alpha, tm=512, tf=1024):
    m, d = x.shape
    dff = w1.shape[2]
    tm = min(tm, seq)
    tf = min(tf, dff)
    per_seq = seq // tm
    n_tiles = m // tm
    steps = dff // tf
    rows = tm // steps
    this, prev, step, done = _tile_grid(n_tiles, steps)
    const = pl.BlockSpec((1, d), lambda i, s: (0, 0))
    return pl.pallas_call(
        functools.partial(_mlp_kernel, alpha=alpha, n_tiles=n_tiles),
        grid=(n_tiles + 1, steps),
        in_specs=[pl.BlockSpec((tm, d), lambda i, s: (this(i), 0)),
                  _mod_vec(d, per_seq, 4, this), _mod_vec(d, per_seq, 3, this),
                  pl.BlockSpec((None, d, tf), lambda i, s: (layer, 0, step(i, s))),
                  pl.BlockSpec((1, tf), lambda i, s: (0, step(i, s))),
                  pl.BlockSpec((None, tf, d), lambda i, s: (layer, step(i, s), 0)),
                  const,
                  pl.BlockSpec((rows, d), lambda i, s: (done(i, s), 0)),
                  _mod_vec(d, per_seq, 5, prev), const, const],
        out_specs=pl.BlockSpec((rows, d), lambda i, s: (done(i, s), 0)),
        out_shape=jax.ShapeDtypeStruct((m, d), F32),
        scratch_shapes=[pltpu.VMEM((tm, d), BF16), pltpu.VMEM((2, tm, d), F32)],
        compiler_params=_params(("arbitrary", "arbitrary")),
        name="mlp",
    )(x, mod3, mod3, w1, b1.reshape(1, dff), w2, b2.reshape(1, d), x, mod3,
      ln_g.reshape(1, d), ln_b.reshape(1, d))


def _conv_residual_ln_kernel(bg_ref, cg_ref, u_ref, cgh_ref, uh_ref, cw_ref, cb_ref, w_ref,
                             xp_ref, gate_ref, lng_ref, lnb_ref, o_ref, p_ref, acc_ref,
                             *, alpha, n_tiles, per_seq, tc):
    steps, _, tk = p_ref.shape
    first = (pl.program_id(0) % per_seq) == 0
    s = pl.program_id(1)

    def start():
        for c0 in range(0, steps * tk, tc):
            cs = slice(c0, c0 + tc)
            gated = cg_ref[:, cs].astype(F32) * u_ref[:, cs].astype(F32)
            halo = cgh_ref[:, cs].astype(F32) * uh_ref[:, cs].astype(F32)
            halo = jnp.where(first, 0.0, halo)
            g = jnp.concatenate([halo, gated], axis=0)
            g1 = pltpu.roll(g, 1, 0)[SUBLANES:]
            g2 = pltpu.roll(g, 2, 0)[SUBLANES:]
            conv = (cw_ref[0:1, cs] * g2 + cw_ref[1:2, cs] * g1 + cw_ref[2:3, cs] * gated
                    + cb_ref[:, cs])
            p_ref[c0 // tk, :, c0 % tk:c0 % tk + tc] = (bg_ref[:, cs].astype(F32) * conv).astype(BF16)

    k0 = pl.multiple_of(s * tk, tk)
    product = lambda: jnp.dot(p_ref[s], w_ref[pl.ds(k0, tk), :], preferred_element_type=F32)
    _tile_step(n_tiles, start, product, None, acc_ref, xp_ref, gate_ref, lng_ref, lnb_ref, o_ref,
               alpha=alpha)


def _conv_residual_ln(bcu, conv_w, conv_b, w_out, layer, x, mod3, ln_g, ln_b, *, seq, alpha,
                      tm=512, steps=2, tc=256):
    m, d = x.shape
    tm = min(tm, seq)
    per_seq = seq // tm
    n_tiles = m // tm
    tk, rows = d // steps, tm // steps
    this, prev, step, done = _tile_grid(n_tiles, steps)
    hb = tm // SUBLANES
    const = lambda r: pl.BlockSpec((r, d), lambda i, s: (0, 0))
    tile = lambda col: pl.BlockSpec((tm, d), lambda i, s: (this(i), col))
    halo = lambda col: pl.BlockSpec((SUBLANES, d),
                                    lambda i, s: (jnp.maximum(this(i) * hb - 1, 0), col))
    return pl.pallas_call(
        functools.partial(_conv_residual_ln_kernel, alpha=alpha, n_tiles=n_tiles, per_seq=per_seq,
                          tc=min(tc, tk)),
        grid=(n_tiles + 1, steps),
        in_specs=[tile(0), tile(1), tile(2), halo(1), halo(2),
                  const(CONV_WIDTH), const(1),
                  pl.BlockSpec((None, d, d), lambda i, s: (layer, 0, 0)),
                  pl.BlockSpec((rows, d), lambda i, s: (done(i, s), 0)),
                  _mod_vec(d, per_seq, 2, prev), const(1), const(1)],
        out_specs=pl.BlockSpec((rows, d), lambda i, s: (done(i, s), 0)),
        out_shape=jax.ShapeDtypeStruct((m, d), F32),
        scratch_shapes=[pltpu.VMEM((steps, tm, tk), BF16), pltpu.VMEM((2, tm, d), F32)],
        compiler_params=_params(("arbitrary", "arbitrary")),
        name="conv_residual_ln",
    )(bcu, bcu, bcu, bcu, bcu, conv_w, conv_b.reshape(1, d), w_out,
      x, mod3, ln_g.reshape(1, d), ln_b.reshape(1, d))


def _attention_kernel(q_ref, k_ref, v_ref, o_ref, z_ref, acc_ref, carry_ref, *, t, dh, scale):
    seq, width = q_ref.shape
    heads = width // dh
    row = lax.broadcasted_iota(jnp.int32, (t, t), 0)
    col = lax.broadcasted_iota(jnp.int32, (t, t), 1)
    causal = col < row
    later = jnp.where(row > col, 1.0, 0.0).astype(BF16)

    def rows(i):
        return pl.ds(pl.multiple_of(i * t, t), t)

    hds = [slice(h * dh, (h + 1) * dh) for h in range(heads)]

    def store_logits(qs, j, h):
        z_ref[h] = lax.dot_general(qs[h], k_ref[rows(j), hds[h]], (((1,), (1,)), ((), ())),
                                   preferred_element_type=F32)

    def key_block(qs, j, diagonal):
        log_betas, l1ms, sticks = [], [], []
        for h in range(heads):
            z = z_ref[h]
            neg_abs = lax.bitcast_convert_type(
                lax.bitcast_convert_type(z, jnp.uint32) | jnp.uint32(0x80000000), F32)
            log_beta = jnp.minimum(z, 0.0) - jnp.log(1.0 + jnp.exp2(neg_abs)) * LOG2_E
            l1m = log_beta - z
            if diagonal:
                l1m = jnp.where(causal, l1m, 0.0)
            sticks.append(jnp.dot(l1m.astype(BF16), later, preferred_element_type=F32))
            log_betas.append(log_beta)
            l1ms.append(l1m)
        for h in range(heads):
            store_logits(qs, jnp.maximum(j - 1, 0), h)
        for h, hd in enumerate(hds):
            carry = carry_ref[h]
            a = jnp.exp2(log_betas[h] + sticks[h] + carry)
            if diagonal:
                a = jnp.where(causal, a, 0.0)
            acc_ref[h] += jnp.dot(a.astype(BF16), v_ref[rows(j), hd], preferred_element_type=F32)
            carry_ref[h] = carry + (sticks[h][:, 0:1] + l1ms[h][:, 0:1])

    def q_tile(qi, _):
        qs = [(q_ref[rows(qi), hd].astype(F32) * (scale * LOG2_E)).astype(BF16) for hd in hds]
        acc_ref[...] = jnp.zeros_like(acc_ref)
        carry_ref[...] = jnp.zeros_like(carry_ref)
        for h in range(heads):
            store_logits(qs, qi, h)
        key_block(qs, qi, True)

        def below_diagonal(i, _):
            key_block(qs, qi - 1 - i, False)
            return 0

        lax.fori_loop(0, qi, below_diagonal, 0)
        for h, hd in enumerate(hds):
            o_ref[rows(qi), hd] = acc_ref[h].astype(o_ref.dtype)
        return 0

    lax.fori_loop(0, seq // t, q_tile, 0)


def _attention(qkv, *, bsz, seq, n_heads, t=256, heads_per_step=4):
    m, d3 = qkv.shape
    d = d3 // 3
    dh = d // n_heads
    t = min(t, seq)
    groups = n_heads // heads_per_step
    width = heads_per_step * dh
    return pl.pallas_call(
        functools.partial(_attention_kernel, t=t, dh=dh, scale=dh ** -0.5),
        grid=(bsz, groups),
        in_specs=[pl.BlockSpec((seq, width), lambda b, g: (b, g)),
                  pl.BlockSpec((seq, width), lambda b, g: (b, groups + g)),
                  pl.BlockSpec((seq, width), lambda b, g: (b, 2 * groups + g))],
        out_specs=pl.BlockSpec((seq, width), lambda b, g: (b, g)),
        out_shape=jax.ShapeDtypeStruct((m, d), BF16),
        scratch_shapes=[pltpu.VMEM((heads_per_step, t, t), F32),
                        pltpu.VMEM((heads_per_step, t, dh), F32),
                        pltpu.VMEM((heads_per_step, t, 1), F32)],
        compiler_params=_params(("arbitrary", "arbitrary")),
        name="stickbreak_attention",
    )(qkv, qkv, qkv)


def _trunk(x, c, mod_w, mod_b, ln_g, ln_b, attn_w_qkv, attn_w_o, conv_w_in, conv_w, conv_b,
           conv_w_out, mlp_w1, mlp_b1, mlp_w2, mlp_b2, *, n_heads):
    bsz, seq, d = x.shape
    depth = mod_w.shape[0]
    alpha = (2.0 * depth) ** 0.25
    xf = x.reshape(bsz * seq, d)
    mod = _modulation(c, mod_w, mod_b)
    w_qkv, w_o = _to_bf16(attn_w_qkv), _to_bf16(attn_w_o)
    w_in, w_out = _to_bf16(conv_w_in), _to_bf16(conv_w_out)
    w1, w2 = _to_bf16(mlp_w1), _to_bf16(mlp_w2)
    for i in range(depth):
        mod3 = mod[i].reshape(bsz * N_MOD, 1, d)
        j = i // 2
        if i % 2 == 0:
            qkv = _mod_matmul(xf, mod3, w_qkv, j, seq=seq)
            o = _attention(qkv, bsz=bsz, seq=seq, n_heads=n_heads)
            xf = _proj_residual_ln(o, w_o, j, xf, mod3, ln_g[i, 0], ln_b[i, 0],
                                   seq=seq, alpha=alpha)
        else:
            bcu = _mod_matmul(xf, mod3, w_in, j, seq=seq)
            xf = _conv_residual_ln(bcu, conv_w[j], conv_b[j], w_out, j, xf, mod3,
                                   ln_g[i, 0], ln_b[i, 0], seq=seq, alpha=alpha)
        xf = _mlp(xf, mod3, w1, mlp_b1[i], w2, mlp_b2[i], i, ln_g[i, 1], ln_b[i, 1],
                  seq=seq, alpha=alpha)
    return xf.reshape(bsz, seq, d)


def kernel(x, c, mod_w, mod_b, ln_g, ln_b, attn_w_qkv, attn_w_o, conv_w_in, conv_w, conv_b,
           conv_w_out, mlp_w1, mlp_b1, mlp_w2, mlp_b2):
    return _trunk(x, c, mod_w, mod_b, ln_g, ln_b, attn_w_qkv, attn_w_o, conv_w_in, conv_w, conv_b,
                  conv_w_out, mlp_w1, mlp_b1, mlp_w2, mlp_b2, n_heads=N_HEADS)
```

```python
import functools

import jax
import jax.numpy as jnp
from jax import lax
from jax.experimental import pallas as pl
from jax.experimental.pallas import tpu as pltpu

N_HEADS = 16
N_MOD = 6
CONV_WIDTH = 3
LN_EPS = 1e-5
LOG2_E = 1.4426950408889634
SUBLANES = 8
VMEM_LIMIT_BYTES = 56 * 1024 * 1024
CAST_BLOCK_ELEMS = 2 * 1024 * 1024

F32 = jnp.float32
BF16 = jnp.bfloat16


def _params(semantics):
    return pltpu.CompilerParams(dimension_semantics=semantics, vmem_limit_bytes=VMEM_LIMIT_BYTES)


def _cast_kernel(w_ref, o_ref):
    o_ref[...] = w_ref[...].astype(o_ref.dtype)


def _to_bf16(w):
    layers, k, n = w.shape
    tk = k
    while tk * n > CAST_BLOCK_ELEMS and tk % 32 == 0:
        tk //= 2
    return pl.pallas_call(
        _cast_kernel,
        grid=(layers, k // tk),
        in_specs=[pl.BlockSpec((1, tk, n), lambda l, i: (l, i, 0))],
        out_specs=pl.BlockSpec((1, tk, n), lambda l, i: (l, i, 0)),
        out_shape=jax.ShapeDtypeStruct(w.shape, BF16),
        compiler_params=_params(("arbitrary", "arbitrary")),
        name="to_bf16",
    )(w)


def _modulation_kernel(c_ref, w_ref, b_ref, o_ref):
    c = c_ref[...]
    cond = c * jax.nn.sigmoid(c)
    o_ref[0] = jnp.dot(cond, w_ref[0], preferred_element_type=F32,
                       precision=lax.Precision.HIGHEST) + b_ref[0]


def _modulation(c, mod_w, mod_b, *, tn=1024):
    depth, d, n = mod_w.shape
    bsz = c.shape[0]
    rows = -(-bsz // SUBLANES) * SUBLANES
    c_pad = jnp.zeros((rows, d), F32).at[:bsz].set(c)
    out = pl.pallas_call(
        _modulation_kernel,
        grid=(depth, n // tn),
        in_specs=[pl.BlockSpec((rows, d), lambda i, j: (0, 0)),
                  pl.BlockSpec((1, d, tn), lambda i, j: (i, 0, j)),
                  pl.BlockSpec((1, 1, tn), lambda i, j: (i, 0, j))],
        out_specs=pl.BlockSpec((1, rows, tn), lambda i, j: (i, 0, j)),
        out_shape=jax.ShapeDtypeStruct((depth, rows, n), F32),
        compiler_params=_params(("arbitrary", "arbitrary")),
        name="modulation",
    )(c_pad, mod_w, mod_b.reshape(depth, 1, n))
    return out[:, :bsz]


def _mod_vec(d, per_seq, j, tile=lambda i: i):
    return pl.BlockSpec((1, 1, d), lambda i, *_: ((tile(i) // per_seq) * N_MOD + j, 0, 0))


def _mod_matmul_kernel(x_ref, sc_ref, sh_ref, w_ref, o_ref, h_ref):
    @pl.when(pl.program_id(1) == 0)
    def _():
        h_ref[...] = (x_ref[...] * (1.0 + sc_ref[0]) + sh_ref[0]).astype(BF16)

    o_ref[...] = jnp.dot(h_ref[...], w_ref[...], preferred_element_type=F32).astype(o_ref.dtype)


def _mod_matmul(x, mod3, w, layer, *, seq, tm=1024, tn=2048):
    m, d = x.shape
    n = w.shape[2]
    tm = min(tm, seq)
    while n % tn:
        tn //= 2
    per_seq = seq // tm
    return pl.pallas_call(
        _mod_matmul_kernel,
        grid=(m // tm, n // tn),
        in_specs=[pl.BlockSpec((tm, d), lambda i, j: (i, 0)),
                  _mod_vec(d, per_seq, 1), _mod_vec(d, per_seq, 0),
                  pl.BlockSpec((None, d, tn), lambda i, j: (layer, 0, j))],
        out_specs=pl.BlockSpec((tm, tn), lambda i, j: (i, j)),
        out_shape=jax.ShapeDtypeStruct((m, n), BF16),
        scratch_shapes=[pltpu.VMEM((tm, d), BF16)],
        compiler_params=_params(("arbitrary", "arbitrary")),
        name="mod_matmul",
    )(x, mod3, mod3, w)


def _residual_ln(y, x, gate_ref, lng_ref, lnb_ref, *, alpha):
    r = alpha * x + (1.0 + gate_ref[0]) * y
    mu = jnp.mean(r, axis=-1, keepdims=True)
    rc = r - mu
    var = jnp.mean(rc * rc, axis=-1, keepdims=True)
    return rc * lax.rsqrt(var + LN_EPS) * lng_ref[...] + lnb_ref[...]


def _tile_step(n_tiles, start, product, bias_ref, acc_ref, xp_ref, gate_ref, lng_ref, lnb_ref, o_ref,
               *, alpha):
    i = pl.program_id(0)
    s = pl.program_id(1)
    cur = i % 2
    rows = o_ref.shape[0]

    def finish_previous_rows():
        y = acc_ref[1 - cur, pl.ds(pl.multiple_of(s * rows, rows), rows), :]
        if bias_ref is not None:
            y = y + bias_ref[...]
        o_ref[...] = _residual_ln(y, xp_ref[...], gate_ref, lng_ref, lnb_ref, alpha=alpha)

    if start is not None:
        pl.when((i < n_tiles) & (s == 0))(start)

    @pl.when((s == 0) & (i == 0))
    def _():
        acc_ref[1] = jnp.zeros(acc_ref.shape[1:], F32)

    @pl.when((i < n_tiles) & (s == 0))
    def _():
        finish_previous_rows()
        acc_ref[cur] = product()

    @pl.when((i < n_tiles) & (s > 0))
    def _():
        finish_previous_rows()
        acc_ref[cur] += product()

    @pl.when(i == n_tiles)
    def _():
        finish_previous_rows()


def _tile_grid(n_tiles, steps):
    this = lambda i: jnp.minimum(i, n_tiles - 1)
    prev = lambda i: jnp.maximum(i - 1, 0)
    step = lambda i, s: jnp.where(i < n_tiles, s, steps - 1)
    done = lambda i, s: jnp.where(i == 0, 0, prev(i) * steps + s)
    return this, prev, step, done


def _proj_residual_ln_kernel(a_ref, w_ref, xp_ref, gate_ref, lng_ref, lnb_ref, o_ref, acc_ref,
                             *, alpha, n_tiles):
    tk = a_ref.shape[1]
    k0 = pl.multiple_of(pl.program_id(1) * tk, tk)
    product = lambda: jnp.dot(a_ref[...], w_ref[pl.ds(k0, tk), :], preferred_element_type=F32)
    _tile_step(n_tiles, None, product, None, acc_ref, xp_ref, gate_ref, lng_ref, lnb_ref, o_ref,
               alpha=alpha)


def _proj_residual_ln(a, w, layer, x, mod3, ln_g, ln_b, *, seq, alpha, tm=512, steps=2):
    m, d = x.shape
    tm = min(tm, seq)
    per_seq = seq // tm
    n_tiles = m // tm
    tk, rows = d // steps, tm // steps
    this, prev, step, done = _tile_grid(n_tiles, steps)
    const = pl.BlockSpec((1, d), lambda i, s: (0, 0))
    return pl.pallas_call(
        functools.partial(_proj_residual_ln_kernel, alpha=alpha, n_tiles=n_tiles),
        grid=(n_tiles + 1, steps),
        in_specs=[pl.BlockSpec((tm, tk), lambda i, s: (this(i), step(i, s))),
                  pl.BlockSpec((None, d, d), lambda i, s: (layer, 0, 0)),
                  pl.BlockSpec((rows, d), lambda i, s: (done(i, s), 0)),
                  _mod_vec(d, per_seq, 2, prev), const, const],
        out_specs=pl.BlockSpec((rows, d), lambda i, s: (done(i, s), 0)),
        out_shape=jax.ShapeDtypeStruct((m, d), F32),
        scratch_shapes=[pltpu.VMEM((2, tm, d), F32)],
        compiler_params=_params(("arbitrary", "arbitrary")),
        name="proj_residual_ln",
    )(a, w, x, mod3, ln_g.reshape(1, d), ln_b.reshape(1, d))


def _mlp_kernel(x_ref, sc_ref, sh_ref, w1_ref, b1_ref, w2_ref, b2_ref, xp_ref, gate_ref, lng_ref,
                lnb_ref, o_ref, h_ref, acc_ref, *, alpha, n_tiles):
    def start():
        h_ref[...] = (x_ref[...] * (1.0 + sc_ref[0]) + sh_ref[0]).astype(BF16)

    def product():
        hid = jnp.dot(h_ref[...], w1_ref[...], preferred_element_type=F32) + b1_ref[...]
        hid = jnp.square(jnp.maximum(hid, 0.0)).astype(BF16)
        return jnp.dot(hid, w2_ref[...], preferred_element_type=F32)

    _tile_step(n_tiles, start, product, b2_ref, acc_ref, xp_ref, gate_ref, lng_ref, lnb_ref, o_ref,
               alpha=alpha)


def _mlp(x, mod3, w1, b1, w2, b2, layer, ln_g, ln_b, *, seq, alpha, tm=512, tf=1024):
    m, d = x.shape
    dff = w1.shape[2]
    tm = min(tm, seq)
    tf = min(tf, dff)
    per_seq = seq // tm
    n_tiles = m // tm
    steps = dff // tf
    rows = tm // steps
    this, prev, step, done = _tile_grid(n_tiles, steps)
    const = pl.BlockSpec((1, d), lambda i, s: (0, 0))
    return pl.pallas_call(
        functools.partial(_mlp_kernel, alpha=alpha, n_tiles=n_tiles),
        grid=(n_tiles + 1, steps),
        in_specs=[pl.BlockSpec((tm, d), lambda i, s: (this(i), 0)),
                  _mod_vec(d, per_seq, 4, this), _mod_vec(d, per_seq, 3, this),
                  pl.BlockSpec((None, d, tf), lambda i, s: (layer, 0, step(i, s))),
                  pl.BlockSpec((1, tf), lambda i, s: (0, step(i, s))),
                  pl.BlockSpec((None, tf, d), lambda i, s: (layer, step(i, s), 0)),
                  const,
                  pl.BlockSpec((rows, d), lambda i, s: (done(i, s), 0)),
                  _mod_vec(d, per_seq, 5, prev), const, const],
        out_specs=pl.BlockSpec((rows, d), lambda i, s: (done(i, s), 0)),
        out_shape=jax.ShapeDtypeStruct((m, d), F32),
        scratch_shapes=[pltpu.VMEM((tm, d), BF16), pltpu.VMEM((2, tm, d), F32)],
        compiler_params=_params(("arbitrary", "arbitrary")),
        name="mlp",
    )(x, mod3, mod3, w1, b1.reshape(1, dff), w2, b2.reshape(1, d), x, mod3,
      ln_g.reshape(1, d), ln_b.reshape(1, d))


def _conv_residual_ln_kernel(bg_ref, cg_ref, u_ref, cgh_ref, uh_ref, cw_ref, cb_ref, w_ref,
                             xp_ref, gate_ref, lng_ref, lnb_ref, o_ref, p_ref, acc_ref,
                             *, alpha, n_tiles, per_seq, tc):
    steps, _, tk = p_ref.shape
    first = (pl.program_id(0) % per_seq) == 0
    s = pl.program_id(1)

    def start():
        for c0 in range(0, steps * tk, tc):
            cs = slice(c0, c0 + tc)
            gated = cg_ref[:, cs].astype(F32) * u_ref[:, cs].astype(F32)
            halo = cgh_ref[:, cs].astype(F32) * uh_ref[:, cs].astype(F32)
            halo = jnp.where(first, 0.0, halo)
            g = jnp.concatenate([halo, gated], axis=0)
            g1 = pltpu.roll(g, 1, 0)[SUBLANES:]
            g2 = pltpu.roll(g, 2, 0)[SUBLANES:]
            conv = (cw_ref[0:1, cs] * g2 + cw_ref[1:2, cs] * g1 + cw_ref[2:3, cs] * gated
                    + cb_ref[:, cs])
            p_ref[c0 // tk, :, c0 % tk:c0 % tk + tc] = (bg_ref[:, cs].astype(F32) * conv).astype(BF16)

    k0 = pl.multiple_of(s * tk, tk)
    product = lambda: jnp.dot(p_ref[s], w_ref[pl.ds(k0, tk), :], preferred_element_type=F32)
    _tile_step(n_tiles, start, product, None, acc_ref, xp_ref, gate_ref, lng_ref, lnb_ref, o_ref,
               alpha=alpha)


def _conv_residual_ln(bcu, conv_w, conv_b, w_out, layer, x, mod3, ln_g, ln_b, *, seq, alpha,
                      tm=512, steps=2, tc=256):
    m, d = x.shape
    tm = min(tm, seq)
    per_seq = seq // tm
    n_tiles = m // tm
    tk, rows = d // steps, tm // steps
    this, prev, step, done = _tile_grid(n_tiles, steps)
    hb = tm // SUBLANES
    const = lambda r: pl.BlockSpec((r, d), lambda i, s: (0, 0))
    tile = lambda col: pl.BlockSpec((tm, d), lambda i, s: (this(i), col))
    halo = lambda col: pl.BlockSpec((SUBLANES, d),
                                    lambda i, s: (jnp.maximum(this(i) * hb - 1, 0), col))
    return pl.pallas_call(
        functools.partial(_conv_residual_ln_kernel, alpha=alpha, n_tiles=n_tiles, per_seq=per_seq,
                          tc=min(tc, tk)),
        grid=(n_tiles + 1, steps),
        in_specs=[tile(0), tile(1), tile(2), halo(1), halo(2),
                  const(CONV_WIDTH), const(1),
                  pl.BlockSpec((None, d, d), lambda i, s: (layer, 0, 0)),
                  pl.BlockSpec((rows, d), lambda i, s: (done(i, s), 0)),
                  _mod_vec(d, per_seq, 2, prev), const(1), const(1)],
        out_specs=pl.BlockSpec((rows, d), lambda i, s: (done(i, s), 0)),
        out_shape=jax.ShapeDtypeStruct((m, d), F32),
        scratch_shapes=[pltpu.VMEM((steps, tm, tk), BF16), pltpu.VMEM((2, tm, d), F32)],
        compiler_params=_params(("arbitrary", "arbitrary")),
        name="conv_residual_ln",
    )(bcu, bcu, bcu, bcu, bcu, conv_w, conv_b.reshape(1, d), w_out,
      x, mod3, ln_g.reshape(1, d), ln_b.reshape(1, d))


def _attention_kernel(q_ref, k_ref, v_ref, o_ref, z_ref, a_ref, acc_ref, carry_ref,
                      *, t, dh, scale):
    seq, width = q_ref.shape
    heads = width // dh
    row = lax.broadcasted_iota(jnp.int32, (t, t), 0)
    col = lax.broadcasted_iota(jnp.int32, (t, t), 1)
    causal = col < row
    later = jnp.where(row > col, 1.0, 0.0).astype(BF16)

    def rows(i):
        return pl.ds(pl.multiple_of(i * t, t), t)

    hds = [slice(h * dh, (h + 1) * dh) for h in range(heads)]

    def store_logits(qs, j, h):
        z_ref[h] = lax.dot_general(qs[h], k_ref[rows(j), hds[h]], (((1,), (1,)), ((), ())),
                                   preferred_element_type=F32)

    def key_block(qs, j, diagonal):
        log_betas, l1ms, sticks = [], [], []
        if not diagonal:
            add_values(j + 1)
        for h in range(heads):
            z = z_ref[h]
            neg_abs = lax.bitcast_convert_type(
                lax.bitcast_convert_type(z, jnp.uint32) | jnp.uint32(0x80000000), F32)
            log_beta = jnp.minimum(z, 0.0) - jnp.log(1.0 + jnp.exp2(neg_abs)) * LOG2_E
            l1m = log_beta - z
            if diagonal:
                l1m = jnp.where(causal, l1m, 0.0)
            sticks.append(jnp.dot(l1m.astype(BF16), later, preferred_element_type=F32))
            log_betas.append(log_beta)
            l1ms.append(l1m)
        for h in range(heads):
            carry = carry_ref[h]
            a = jnp.exp2(log_betas[h] + sticks[h] + carry)
            if diagonal:
                a = jnp.where(causal, a, 0.0)
            a_ref[h] = a.astype(BF16)
            carry_ref[h] = carry + (sticks[h][:, 0:1] + l1ms[h][:, 0:1])
        for h in range(heads):
            store_logits(qs, jnp.maximum(j - 1, 0), h)

    def add_values(j):
        for h, hd in enumerate(hds):
            acc_ref[h] += jnp.dot(a_ref[h], v_ref[rows(j), hd], preferred_element_type=F32)

    def q_tile(qi, _):
        qs = [(q_ref[rows(qi), hd].astype(F32) * (scale * LOG2_E)).astype(BF16) for hd in hds]
        acc_ref[...] = jnp.zeros_like(acc_ref)
        carry_ref[...] = jnp.zeros_like(carry_ref)
        for h in range(heads):
            store_logits(qs, qi, h)
        key_block(qs, qi, True)

        def below_diagonal(i, _):
            key_block(qs, qi - 1 - i, False)
            return 0

        lax.fori_loop(0, qi, below_diagonal, 0)
        add_values(0)
        for h, hd in enumerate(hds):
            o_ref[rows(qi), hd] = acc_ref[h].astype(o_ref.dtype)
        return 0

    lax.fori_loop(0, seq // t, q_tile, 0)


def _attention(qkv, *, bsz, seq, n_heads, t=256, heads_per_step=4):
    m, d3 = qkv.shape
    d = d3 // 3
    dh = d // n_heads
    t = min(t, seq)
    groups = n_heads // heads_per_step
    width = heads_per_step * dh
    return pl.pallas_call(
        functools.partial(_attention_kernel, t=t, dh=dh, scale=dh ** -0.5),
        grid=(bsz, groups),
        in_specs=[pl.BlockSpec((seq, width), lambda b, g: (b, g)),
                  pl.BlockSpec((seq, width), lambda b, g: (b, groups + g)),
                  pl.BlockSpec((seq, width), lambda b, g: (b, 2 * groups + g))],
        out_specs=pl.BlockSpec((seq, width), lambda b, g: (b, g)),
        out_shape=jax.ShapeDtypeStruct((m, d), BF16),
        scratch_shapes=[pltpu.VMEM((heads_per_step, t, t), F32),
                        pltpu.VMEM((heads_per_step, t, t), BF16),
                        pltpu.VMEM((heads_per_step, t, dh), F32),
                        pltpu.VMEM((heads_per_step, t, 1), F32)],
        compiler_params=_params(("arbitrary", "arbitrary")),
        name="stickbreak_attention",
    )(qkv, qkv, qkv)


def _trunk(x, c, mod_w, mod_b, ln_g, ln_b, attn_w_qkv, attn_w_o, conv_w_in, conv_w, conv_b,
           conv_w_out, mlp_w1, mlp_b1, mlp_w2, mlp_b2, *, n_heads):
    bsz, seq, d = x.shape
    depth = mod_w.shape[0]
    alpha = (2.0 * depth) ** 0.25
    xf = x.reshape(bsz * seq, d)
    mod = _modulation(c, mod_w, mod_b)
    w_qkv, w_o = _to_bf16(attn_w_qkv), _to_bf16(attn_w_o)
    w_in, w_out = _to_bf16(conv_w_in), _to_bf16(conv_w_out)
    w1, w2 = _to_bf16(mlp_w1), _to_bf16(mlp_w2)
    for i in range(depth):
        mod3 = mod[i].reshape(bsz * N_MOD, 1, d)
        j = i // 2
        if i % 2 == 0:
            qkv = _mod_matmul(xf, mod3, w_qkv, j, seq=seq)
            o = _attention(qkv, bsz=bsz, seq=seq, n_heads=n_heads)
            xf = _proj_residual_ln(o, w_o, j, xf, mod3, ln_g[i, 0], ln_b[i, 0],
                                   seq=seq, alpha=alpha)
        else:
            bcu = _mod_matmul(xf, mod3, w_in, j, seq=seq)
            xf = _conv_residual_ln(bcu, conv_w[j], conv_b[j], w_out, j, xf, mod3,
                                   ln_g[i, 0], ln_b[i, 0], seq=seq, alpha=alpha)
        xf = _mlp(xf, mod3, w1, mlp_b1[i], w2, mlp_b2[i], i, ln_g[i, 1], ln_b[i, 1],
                  seq=seq, alpha=alpha)
    return xf.reshape(bsz, seq, d)


def kernel(x, c, mod_w, mod_b, ln_g, ln_b, attn_w_qkv, attn_w_o, conv_w_in, conv_w, conv_b,
           conv_w_out, mlp_w1, mlp_b1, mlp_w2, mlp_b2):
    return _trunk(x, c, mod_w, mod_b, ln_g, ln_b, attn_w_qkv, attn_w_o, conv_w_in, conv_w, conv_b,
                  conv_w_out, mlp_w1, mlp_b1, mlp_w2, mlp_b2, n_heads=N_HEADS)
```

```python
import functools

import jax
import jax.numpy as jnp
from jax import lax
from jax.experimental import pallas as pl
from jax.experimental.pallas import tpu as pltpu

N_HEADS = 16
N_MOD = 6
CONV_WIDTH = 3
LN_EPS = 1e-5
LOG2_E = 1.4426950408889634
SUBLANES = 8
VMEM_LIMIT_BYTES = 56 * 1024 * 1024
CAST_BLOCK_ELEMS = 2 * 1024 * 1024

F32 = jnp.float32
BF16 = jnp.bfloat16


def _params(semantics):
    return pltpu.CompilerParams(dimension_semantics=semantics, vmem_limit_bytes=VMEM_LIMIT_BYTES)


def _cast_kernel(w_ref, o_ref):
    o_ref[...] = w_ref[...].astype(o_ref.dtype)


def _to_bf16(w):
    layers, k, n = w.shape
    tk = k
    while tk * n > CAST_BLOCK_ELEMS and tk % 32 == 0:
        tk //= 2
    return pl.pallas_call(
        _cast_kernel,
        grid=(layers, k // tk),
        in_specs=[pl.BlockSpec((1, tk, n), lambda l, i: (l, i, 0))],
        out_specs=pl.BlockSpec((1, tk, n), lambda l, i: (l, i, 0)),
        out_shape=jax.ShapeDtypeStruct(w.shape, BF16),
        compiler_params=_params(("arbitrary", "arbitrary")),
        name="to_bf16",
    )(w)


def _modulation_kernel(c_ref, w_ref, b_ref, o_ref):
    c = c_ref[...]
    cond = c * jax.nn.sigmoid(c)
    o_ref[0] = jnp.dot(cond, w_ref[0], preferred_element_type=F32,
                       precision=lax.Precision.HIGHEST) + b_ref[0]


def _modulation(c, mod_w, mod_b, *, tn=1024):
    depth, d, n = mod_w.shape
    bsz = c.shape[0]
    rows = -(-bsz // SUBLANES) * SUBLANES
    c_pad = jnp.zeros((rows, d), F32).at[:bsz].set(c)
    out = pl.pallas_call(
        _modulation_kernel,
        grid=(depth, n // tn),
        in_specs=[pl.BlockSpec((rows, d), lambda i, j: (0, 0)),
                  pl.BlockSpec((1, d, tn), lambda i, j: (i, 0, j)),
                  pl.BlockSpec((1, 1, tn), lambda i, j: (i, 0, j))],
        out_specs=pl.BlockSpec((1, rows, tn), lambda i, j: (i, 0, j)),
        out_shape=jax.ShapeDtypeStruct((depth, rows, n), F32),
        compiler_params=_params(("arbitrary", "arbitrary")),
        name="modulation",
    )(c_pad, mod_w, mod_b.reshape(depth, 1, n))
    return out[:, :bsz]


def _mod_vec(d, per_seq, j, tile=lambda i: i):
    return pl.BlockSpec((1, 1, d), lambda i, *_: ((tile(i) // per_seq) * N_MOD + j, 0, 0))


def _mod_matmul_kernel(x_ref, sc_ref, sh_ref, w_ref, o_ref, h_ref):
    @pl.when(pl.program_id(1) == 0)
    def _():
        h_ref[...] = (x_ref[...] * (1.0 + sc_ref[0]) + sh_ref[0]).astype(BF16)

    o_ref[...] = jnp.dot(h_ref[...], w_ref[...], preferred_element_type=F32).astype(o_ref.dtype)


def _mod_matmul(x, mod3, w, layer, *, seq, tm=1024, tn=2048):
    m, d = x.shape
    n = w.shape[2]
    tm = min(tm, seq)
    while n % tn:
        tn //= 2
    per_seq = seq // tm
    return pl.pallas_call(
        _mod_matmul_kernel,
        grid=(m // tm, n // tn),
        in_specs=[pl.BlockSpec((tm, d), lambda i, j: (i, 0)),
                  _mod_vec(d, per_seq, 1), _mod_vec(d, per_seq, 0),
                  pl.BlockSpec((None, d, tn), lambda i, j: (layer, 0, j))],
        out_specs=pl.BlockSpec((tm, tn), lambda i, j: (i, j)),
        out_shape=jax.ShapeDtypeStruct((m, n), BF16),
        scratch_shapes=[pltpu.VMEM((tm, d), BF16)],
        compiler_params=_params(("arbitrary", "arbitrary")),
        name="mod_matmul",
    )(x, mod3, mod3, w)


def _residual_ln(y, x, gate_ref, lng_ref, lnb_ref, *, alpha):
    r = alpha * x + (1.0 + gate_ref[0]) * y
    mu = jnp.mean(r, axis=-1, keepdims=True)
    rc = r - mu
    var = jnp.mean(rc * rc, axis=-1, keepdims=True)
    return rc * lax.rsqrt(var + LN_EPS) * lng_ref[...] + lnb_ref[...]


def _tile_step(n_tiles, start, product, bias_ref, acc_ref, xp_ref, gate_ref, lng_ref, lnb_ref, o_ref,
               *, alpha):
    i = pl.program_id(0)
    s = pl.program_id(1)
    cur = i % 2
    rows = o_ref.shape[0]

    def finish_previous_rows():
        y = acc_ref[1 - cur, pl.ds(pl.multiple_of(s * rows, rows), rows), :]
        if bias_ref is not None:
            y = y + bias_ref[...]
        o_ref[...] = _residual_ln(y, xp_ref[...], gate_ref, lng_ref, lnb_ref, alpha=alpha)

    if start is not None:
        pl.when((i < n_tiles) & (s == 0))(start)

    @pl.when((s == 0) & (i == 0))
    def _():
        acc_ref[1] = jnp.zeros(acc_ref.shape[1:], F32)

    @pl.when((i < n_tiles) & (s == 0))
    def _():
        finish_previous_rows()
        acc_ref[cur] = product()

    @pl.when((i < n_tiles) & (s > 0))
    def _():
        finish_previous_rows()
        acc_ref[cur] += product()

    @pl.when(i == n_tiles)
    def _():
        finish_previous_rows()


def _tile_grid(n_tiles, steps):
    this = lambda i: jnp.minimum(i, n_tiles - 1)
    prev = lambda i: jnp.maximum(i - 1, 0)
    step = lambda i, s: jnp.where(i < n_tiles, s, steps - 1)
    done = lambda i, s: jnp.where(i == 0, 0, prev(i) * steps + s)
    return this, prev, step, done


def _proj_residual_ln_kernel(a_ref, w_ref, xp_ref, gate_ref, lng_ref, lnb_ref, o_ref, acc_ref,
                             *, alpha, n_tiles):
    tk = a_ref.shape[1]
    k0 = pl.multiple_of(pl.program_id(1) * tk, tk)
    product = lambda: jnp.dot(a_ref[...], w_ref[pl.ds(k0, tk), :], preferred_element_type=F32)
    _tile_step(n_tiles, None, product, None, acc_ref, xp_ref, gate_ref, lng_ref, lnb_ref, o_ref,
               alpha=alpha)


def _proj_residual_ln(a, w, layer, x, mod3, ln_g, ln_b, *, seq, alpha, tm=512, steps=2):
    m, d = x.shape
    tm = min(tm, seq)
    per_seq = seq // tm
    n_tiles = m // tm
    tk, rows = d // steps, tm // steps
    this, prev, step, done = _tile_grid(n_tiles, steps)
    const = pl.BlockSpec((1, d), lambda i, s: (0, 0))
    return pl.pallas_call(
        functools.partial(_proj_residual_ln_kernel, alpha=alpha, n_tiles=n_tiles),
        grid=(n_tiles + 1, steps),
        in_specs=[pl.BlockSpec((tm, tk), lambda i, s: (this(i), step(i, s))),
                  pl.BlockSpec((None, d, d), lambda i, s: (layer, 0, 0)),
                  pl.BlockSpec((rows, d), lambda i, s: (done(i, s), 0)),
                  _mod_vec(d, per_seq, 2, prev), const, const],
        out_specs=pl.BlockSpec((rows, d), lambda i, s: (done(i, s), 0)),
        out_shape=jax.ShapeDtypeStruct((m, d), F32),
        scratch_shapes=[pltpu.VMEM((2, tm, d), F32)],
        compiler_params=_params(("arbitrary", "arbitrary")),
        name="proj_residual_ln",
    )(a, w, x, mod3, ln_g.reshape(1, d), ln_b.reshape(1, d))


def _mlp_kernel(x_ref, sc_ref, sh_ref, w1_ref, b1_ref, w2_ref, b2_ref, xp_ref, gate_ref, lng_ref,
                lnb_ref, o_ref, h_ref, acc_ref, *, alpha, n_tiles):
    def start():
        h_ref[...] = (x_ref[...] * (1.0 + sc_ref[0]) + sh_ref[0]).astype(BF16)

    def product():
        hid = jnp.dot(h_ref[...], w1_ref[...], preferred_element_type=F32) + b1_ref[...]
        hid = jnp.square(jnp.maximum(hid, 0.0)).astype(BF16)
        return jnp.dot(hid, w2_ref[...], preferred_element_type=F32)

    _tile_step(n_tiles, start, product, b2_ref, acc_ref, xp_ref, gate_ref, lng_ref, lnb_ref, o_ref,
               alpha=alpha)


def _mlp(x, mod3, w1, b1, w2, b2, layer, ln_g, ln_b, *, seq, alpha, tm=512, tf=1024):
    m, d = x.shape
    dff = w1.shape[2]
    tm = min(tm, seq)
    tf = min(tf, dff)
    per_seq = seq // tm
    n_tiles = m // tm
    steps = dff // tf
    rows = tm // steps
    this, prev, step, done = _tile_grid(n_tiles, steps)
    const = pl.BlockSpec((1, d), lambda i, s: (0, 0))
    return pl.pallas_call(
        functools.partial(_mlp_kernel, alpha=alpha, n_tiles=n_tiles),
        grid=(n_tiles + 1, steps),
        in_specs=[pl.BlockSpec((tm, d), lambda i, s: (this(i), 0)),
                  _mod_vec(d, per_seq, 4, this), _mod_vec(d, per_seq, 3, this),
                  pl.BlockSpec((None, d, tf), lambda i, s: (layer, 0, step(i, s))),
                  pl.BlockSpec((1, tf), lambda i, s: (0, step(i, s))),
                  pl.BlockSpec((None, tf, d), lambda i, s: (layer, step(i, s), 0)),
                  const,
                  pl.BlockSpec((rows, d), lambda i, s: (done(i, s), 0)),
                  _mod_vec(d, per_seq, 5, prev), const, const],
        out_specs=pl.BlockSpec((rows, d), lambda i, s: (done(i, s), 0)),
        out_shape=jax.ShapeDtypeStruct((m, d), F32),
        scratch_shapes=[pltpu.VMEM((tm, d), BF16), pltpu.VMEM((2, tm, d), F32)],
        compiler_params=_params(("arbitrary", "arbitrary")),
        name="mlp",
    )(x, mod3, mod3, w1, b1.reshape(1, dff), w2, b2.reshape(1, d), x, mod3,
      ln_g.reshape(1, d), ln_b.reshape(1, d))


def _conv_residual_ln_kernel(bg_ref, cg_ref, u_ref, cgh_ref, uh_ref, cw_ref, cb_ref, w_ref,
                             xp_ref, gate_ref, lng_ref, lnb_ref, o_ref, p_ref, acc_ref,
                             *, alpha, n_tiles, per_seq, tc):
    steps, _, tk = p_ref.shape
    first = (pl.program_id(0) % per_seq) == 0
    s = pl.program_id(1)

    def start():
        for c0 in range(0, steps * tk, tc):
            cs = slice(c0, c0 + tc)
            gated = cg_ref[:, cs].astype(F32) * u_ref[:, cs].astype(F32)
            halo = cgh_ref[:, cs].astype(F32) * uh_ref[:, cs].astype(F32)
            halo = jnp.where(first, 0.0, halo)
            g = jnp.concatenate([halo, gated], axis=0)
            g1 = pltpu.roll(g, 1, 0)[SUBLANES:]
            g2 = pltpu.roll(g, 2, 0)[SUBLANES:]
            conv = (cw_ref[0:1, cs] * g2 + cw_ref[1:2, cs] * g1 + cw_ref[2:3, cs] * gated
                    + cb_ref[:, cs])
            p_ref[c0 // tk, :, c0 % tk:c0 % tk + tc] = (bg_ref[:, cs].astype(F32) * conv).astype(BF16)

    k0 = pl.multiple_of(s * tk, tk)
    product = lambda: jnp.dot(p_ref[s], w_ref[pl.ds(k0, tk), :], preferred_element_type=F32)
    _tile_step(n_tiles, start, product, None, acc_ref, xp_ref, gate_ref, lng_ref, lnb_ref, o_ref,
               alpha=alpha)


def _conv_residual_ln(bcu, conv_w, conv_b, w_out, layer, x, mod3, ln_g, ln_b, *, seq, alpha,
                      tm=512, steps=2, tc=256):
    m, d = x.shape
    tm = min(tm, seq)
    per_seq = seq // tm
    n_tiles = m // tm
    tk, rows = d // steps, tm // steps
    this, prev, step, done = _tile_grid(n_tiles, steps)
    hb = tm // SUBLANES
    const = lambda r: pl.BlockSpec((r, d), lambda i, s: (0, 0))
    tile = lambda col: pl.BlockSpec((tm, d), lambda i, s: (this(i), col))
    halo = lambda col: pl.BlockSpec((SUBLANES, d),
                                    lambda i, s: (jnp.maximum(this(i) * hb - 1, 0), col))
    return pl.pallas_call(
        functools.partial(_conv_residual_ln_kernel, alpha=alpha, n_tiles=n_tiles, per_seq=per_seq,
                          tc=min(tc, tk)),
        grid=(n_tiles + 1, steps),
        in_specs=[tile(0), tile(1), tile(2), halo(1), halo(2),
                  const(CONV_WIDTH), const(1),
                  pl.BlockSpec((None, d, d), lambda i, s: (layer, 0, 0)),
                  pl.BlockSpec((rows, d), lambda i, s: (done(i, s), 0)),
                  _mod_vec(d, per_seq, 2, prev), const(1), const(1)],
        out_specs=pl.BlockSpec((rows, d), lambda i, s: (done(i, s), 0)),
        out_shape=jax.ShapeDtypeStruct((m, d), F32),
        scratch_shapes=[pltpu.VMEM((steps, tm, tk), BF16), pltpu.VMEM((2, tm, d), F32)],
        compiler_params=_params(("arbitrary", "arbitrary")),
        name="conv_residual_ln",
    )(bcu, bcu, bcu, bcu, bcu, conv_w, conv_b.reshape(1, d), w_out,
      x, mod3, ln_g.reshape(1, d), ln_b.reshape(1, d))


def _attention_kernel(q_ref, k_ref, v_ref, o_ref, z_ref, acc_ref, carry_ref, *, t, dh, scale):
    seq, width = q_ref.shape
    heads = width // dh
    row = lax.broadcasted_iota(jnp.int32, (t, t), 0)
    col = lax.broadcasted_iota(jnp.int32, (t, t), 1)
    causal = col < row
    later = jnp.where(row > col, 1.0, 0.0).astype(BF16)

    def rows(i):
        return pl.ds(pl.multiple_of(i * t, t), t)

    hds = [slice(h * dh, (h + 1) * dh) for h in range(heads)]

    def store_logits(qs, j, h):
        z_ref[h] = lax.dot_general(qs[h], k_ref[rows(j), hds[h]], (((1,), (1,)), ((), ())),
                                   preferred_element_type=F32)

    def key_block(qs, j, diagonal):
        log_betas, l1ms, sticks = [], [], []
        for h in range(heads):
            z = z_ref[h]
            neg_abs = lax.bitcast_convert_type(
                lax.bitcast_convert_type(z, jnp.uint32) | jnp.uint32(0x80000000), F32)
            log_beta = jnp.minimum(z, 0.0) - jnp.log(1.0 + jnp.exp2(neg_abs)) * LOG2_E
            l1m = log_beta - z
            if diagonal:
                l1m = jnp.where(causal, l1m, 0.0)
            sticks.append(jnp.dot(l1m.astype(BF16), later, preferred_element_type=F32))
            log_betas.append(log_beta)
            l1ms.append(l1m)
        for h in range(heads):
            store_logits(qs, jnp.maximum(j - 1, 0), h)
        for h, hd in enumerate(hds):
            carry = carry_ref[h]
            a = jnp.exp2(log_betas[h] + sticks[h])
            if diagonal:
                a = jnp.where(causal, a, 0.0)
            av = jnp.dot(a.astype(BF16), v_ref[rows(j), hd], preferred_element_type=F32)
            acc_ref[h] += jnp.exp2(carry) * av
            carry_ref[h] = carry + (sticks[h][:, 0:1] + l1ms[h][:, 0:1])

    def q_tile(qi, _):
        qs = [(q_ref[rows(qi), hd].astype(F32) * (scale * LOG2_E)).astype(BF16) for hd in hds]
        acc_ref[...] = jnp.zeros_like(acc_ref)
        carry_ref[...] = jnp.zeros_like(carry_ref)
        for h in range(heads):
            store_logits(qs, qi, h)
        key_block(qs, qi, True)

        def below_diagonal(i, _):
            key_block(qs, qi - 1 - i, False)
            return 0

        lax.fori_loop(0, qi, below_diagonal, 0)
        for h, hd in enumerate(hds):
            o_ref[rows(qi), hd] = acc_ref[h].astype(o_ref.dtype)
        return 0

    lax.fori_loop(0, seq // t, q_tile, 0)


def _attention(qkv, *, bsz, seq, n_heads, t=256, heads_per_step=4):
    m, d3 = qkv.shape
    d = d3 // 3
    dh = d // n_heads
    t = min(t, seq)
    groups = n_heads // heads_per_step
    width = heads_per_step * dh
    return pl.pallas_call(
        functools.partial(_attention_kernel, t=t, dh=dh, scale=dh ** -0.5),
        grid=(bsz, groups),
        in_specs=[pl.BlockSpec((seq, width), lambda b, g: (b, g)),
                  pl.BlockSpec((seq, width), lambda b, g: (b, groups + g)),
                  pl.BlockSpec((seq, width), lambda b, g: (b, 2 * groups + g))],
        out_specs=pl.BlockSpec((seq, width), lambda b, g: (b, g)),
        out_shape=jax.ShapeDtypeStruct((m, d), BF16),
        scratch_shapes=[pltpu.VMEM((heads_per_step, t, t), F32),
                        pltpu.VMEM((heads_per_step, t, dh), F32),
                        pltpu.VMEM((heads_per_step, t, 1), F32)],
        compiler_params=_params(("arbitrary", "arbitrary")),
        name="stickbreak_attention",
    )(qkv, qkv, qkv)


def _trunk(x, c, mod_w, mod_b, ln_g, ln_b, attn_w_qkv, attn_w_o, conv_w_in, conv_w, conv_b,
           conv_w_out, mlp_w1, mlp_b1, mlp_w2, mlp_b2, *, n_heads):
    bsz, seq, d = x.shape
    depth = mod_w.shape[0]
    alpha = (2.0 * depth) ** 0.25
    xf = x.reshape(bsz * seq, d)
    mod = _modulation(c, mod_w, mod_b)
    w_qkv, w_o = _to_bf16(attn_w_qkv), _to_bf16(attn_w_o)
    w_in, w_out = _to_bf16(conv_w_in), _to_bf16(conv_w_out)
    w1, w2 = _to_bf16(mlp_w1), _to_bf16(mlp_w2)
    for i in range(depth):
        mod3 = mod[i].reshape(bsz * N_MOD, 1, d)
        j = i // 2
        if i % 2 == 0:
            qkv = _mod_matmul(xf, mod3, w_qkv, j, seq=seq)
            o = _attention(qkv, bsz=bsz, seq=seq, n_heads=n_heads)
            xf = _proj_residual_ln(o, w_o, j, xf, mod3, ln_g[i, 0], ln_b[i, 0],
                                   seq=seq, alpha=alpha)
        else:
            bcu = _mod_matmul(xf, mod3, w_in, j, seq=seq)
            xf = _conv_residual_ln(bcu, conv_w[j], conv_b[j], w_out, j, xf, mod3,
                                   ln_g[i, 0], ln_b[i, 0], seq=seq, alpha=alpha)
        xf = _mlp(xf, mod3, w1, mlp_b1[i], w2, mlp_b2[i], i, ln_g[i, 1], ln_b[i, 1],
                  seq=seq, alpha=alpha)
    return xf.reshape(bsz, seq, d)


def kernel(x, c, mod_w, mod_b, ln_g, ln_b, attn_w_qkv, attn_w_o, conv_w_in, conv_w, conv_b,
           conv_w_out, mlp_w1, mlp_b1, mlp_w2, mlp_b2):
    return _trunk(x, c, mod_w, mod_b, ln_g, ln_b, attn_w_qkv, attn_w_o, conv_w_in, conv_w, conv_b,
                  conv_w_out, mlp_w1, mlp_b1, mlp_w2, mlp_b2, n_heads=N_HEADS)
```

```python
import functools

import jax
import jax.numpy as jnp
from jax import lax
from jax.experimental import pallas as pl
from jax.experimental.pallas import tpu as pltpu

N_HEADS = 16
N_MOD = 6
CONV_WIDTH = 3
LN_EPS = 1e-5
LOG2_E = 1.4426950408889634
SUBLANES = 8
VMEM_LIMIT_BYTES = 56 * 1024 * 1024
CAST_BLOCK_ELEMS = 2 * 1024 * 1024

F32 = jnp.float32
BF16 = jnp.bfloat16


def _params(semantics):
    return pltpu.CompilerParams(dimension_semantics=semantics, vmem_limit_bytes=VMEM_LIMIT_BYTES)


def _cast_kernel(w_ref, o_ref):
    o_ref[...] = w_ref[...].astype(o_ref.dtype)


def _to_bf16(w):
    layers, k, n = w.shape
    tk = k
    while tk * n > CAST_BLOCK_ELEMS and tk % 32 == 0:
        tk //= 2
    return pl.pallas_call(
        _cast_kernel,
        grid=(layers, k // tk),
        in_specs=[pl.BlockSpec((1, tk, n), lambda l, i: (l, i, 0))],
        out_specs=pl.BlockSpec((1, tk, n), lambda l, i: (l, i, 0)),
        out_shape=jax.ShapeDtypeStruct(w.shape, BF16),
        compiler_params=_params(("arbitrary", "arbitrary")),
        name="to_bf16",
    )(w)


def _modulation_kernel(c_ref, w_ref, b_ref, o_ref):
    c = c_ref[...]
    cond = c * jax.nn.sigmoid(c)
    o_ref[0] = jnp.dot(cond, w_ref[0], preferred_element_type=F32,
                       precision=lax.Precision.HIGHEST) + b_ref[0]


def _modulation(c, mod_w, mod_b, *, tn=1024):
    depth, d, n = mod_w.shape
    bsz = c.shape[0]
    rows = -(-bsz // SUBLANES) * SUBLANES
    c_pad = jnp.zeros((rows, d), F32).at[:bsz].set(c)
    out = pl.pallas_call(
        _modulation_kernel,
        grid=(depth, n // tn),
        in_specs=[pl.BlockSpec((rows, d), lambda i, j: (0, 0)),
                  pl.BlockSpec((1, d, tn), lambda i, j: (i, 0, j)),
                  pl.BlockSpec((1, 1, tn), lambda i, j: (i, 0, j))],
        out_specs=pl.BlockSpec((1, rows, tn), lambda i, j: (i, 0, j)),
        out_shape=jax.ShapeDtypeStruct((depth, rows, n), F32),
        compiler_params=_params(("arbitrary", "arbitrary")),
        name="modulation",
    )(c_pad, mod_w, mod_b.reshape(depth, 1, n))
    return out[:, :bsz]


def _mod_vec(d, per_seq, j, tile=lambda i: i):
    return pl.BlockSpec((1, 1, d), lambda i, *_: ((tile(i) // per_seq) * N_MOD + j, 0, 0))


def _mod_matmul_kernel(x_ref, sc_ref, sh_ref, w_ref, o_ref, h_ref):
    @pl.when(pl.program_id(1) == 0)
    def _():
        h_ref[...] = (x_ref[...] * (1.0 + sc_ref[0]) + sh_ref[0]).astype(BF16)

    o_ref[...] = jnp.dot(h_ref[...], w_ref[...], preferred_element_type=F32).astype(o_ref.dtype)


def _mod_matmul(x, mod3, w, layer, *, seq, tm=1024, tn=2048):
    m, d = x.shape
    n = w.shape[2]
    tm = min(tm, seq)
    while n % tn:
        tn //= 2
    per_seq = seq // tm
    return pl.pallas_call(
        _mod_matmul_kernel,
        grid=(m // tm, n // tn),
        in_specs=[pl.BlockSpec((tm, d), lambda i, j: (i, 0)),
                  _mod_vec(d, per_seq, 1), _mod_vec(d, per_seq, 0),
                  pl.BlockSpec((None, d, tn), lambda i, j: (layer, 0, j))],
        out_specs=pl.BlockSpec((tm, tn), lambda i, j: (i, j)),
        out_shape=jax.ShapeDtypeStruct((m, n), BF16),
        scratch_shapes=[pltpu.VMEM((tm, d), BF16)],
        compiler_params=_params(("arbitrary", "arbitrary")),
        name="mod_matmul",
    )(x, mod3, mod3, w)


def _residual_ln(y, x, gate_ref, lng_ref, lnb_ref, *, alpha):
    r = alpha * x + (1.0 + gate_ref[0]) * y
    mu = jnp.mean(r, axis=-1, keepdims=True)
    rc = r - mu
    var = jnp.mean(rc * rc, axis=-1, keepdims=True)
    return rc * lax.rsqrt(var + LN_EPS) * lng_ref[...] + lnb_ref[...]


def _tile_step(n_tiles, start, product, bias_ref, acc_ref, xp_ref, gate_ref, lng_ref, lnb_ref, o_ref,
               next_in=None, *, alpha):
    i = pl.program_id(0)
    s = pl.program_id(1)
    cur = i % 2
    rows = o_ref.shape[0]

    def finish_previous_rows():
        y = acc_ref[1 - cur, pl.ds(pl.multiple_of(s * rows, rows), rows), :]
        if bias_ref is not None:
            y = y + bias_ref[...]
        x_new = _residual_ln(y, xp_ref[...], gate_ref, lng_ref, lnb_ref, alpha=alpha)
        o_ref[...] = x_new
        if next_in is not None:
            sc_ref, sh_ref, h_ref = next_in
            h_ref[...] = (x_new * (1.0 + sc_ref[0]) + sh_ref[0]).astype(BF16)

    if start is not None:
        pl.when((i < n_tiles) & (s == 0))(start)

    @pl.when((s == 0) & (i == 0))
    def _():
        acc_ref[1] = jnp.zeros(acc_ref.shape[1:], F32)

    @pl.when((i < n_tiles) & (s == 0))
    def _():
        finish_previous_rows()
        acc_ref[cur] = product()

    @pl.when((i < n_tiles) & (s > 0))
    def _():
        finish_previous_rows()
        acc_ref[cur] += product()

    @pl.when(i == n_tiles)
    def _():
        finish_previous_rows()


def _tile_grid(n_tiles, steps):
    this = lambda i: jnp.minimum(i, n_tiles - 1)
    prev = lambda i: jnp.maximum(i - 1, 0)
    step = lambda i, s: jnp.where(i < n_tiles, s, steps - 1)
    done = lambda i, s: jnp.where(i == 0, 0, prev(i) * steps + s)
    return this, prev, step, done


def _proj_residual_ln_kernel(a_ref, w_ref, xp_ref, gate_ref, lng_ref, lnb_ref, sc_ref, sh_ref,
                             o_ref, h_ref, acc_ref, *, alpha, n_tiles):
    tk = a_ref.shape[1]
    k0 = pl.multiple_of(pl.program_id(1) * tk, tk)
    product = lambda: jnp.dot(a_ref[...], w_ref[pl.ds(k0, tk), :], preferred_element_type=F32)
    _tile_step(n_tiles, None, product, None, acc_ref, xp_ref, gate_ref, lng_ref, lnb_ref, o_ref,
               (sc_ref, sh_ref, h_ref), alpha=alpha)


def _proj_residual_ln(a, w, layer, x, mod3, ln_g, ln_b, *, seq, alpha, tm=512, steps=2):
    m, d = x.shape
    tm = min(tm, seq)
    per_seq = seq // tm
    n_tiles = m // tm
    tk, rows = d // steps, tm // steps
    this, prev, step, done = _tile_grid(n_tiles, steps)
    const = pl.BlockSpec((1, d), lambda i, s: (0, 0))
    rows_spec = pl.BlockSpec((rows, d), lambda i, s: (done(i, s), 0))
    return pl.pallas_call(
        functools.partial(_proj_residual_ln_kernel, alpha=alpha, n_tiles=n_tiles),
        grid=(n_tiles + 1, steps),
        in_specs=[pl.BlockSpec((tm, tk), lambda i, s: (this(i), step(i, s))),
                  pl.BlockSpec((None, d, d), lambda i, s: (layer, 0, 0)),
                  rows_spec,
                  _mod_vec(d, per_seq, 2, prev), const, const,
                  _mod_vec(d, per_seq, 4, prev), _mod_vec(d, per_seq, 3, prev)],
        out_specs=[rows_spec, rows_spec],
        out_shape=[jax.ShapeDtypeStruct((m, d), F32), jax.ShapeDtypeStruct((m, d), BF16)],
        scratch_shapes=[pltpu.VMEM((2, tm, d), F32)],
        compiler_params=_params(("arbitrary", "arbitrary")),
        name="proj_residual_ln",
    )(a, w, x, mod3, ln_g.reshape(1, d), ln_b.reshape(1, d), mod3, mod3)


def _mlp_kernel(h_ref, w1_ref, b1_ref, w2_ref, b2_ref, xp_ref, gate_ref, lng_ref, lnb_ref,
                o_ref, acc_ref, *, alpha, n_tiles):
    def product():
        hid = jnp.dot(h_ref[...], w1_ref[...], preferred_element_type=F32) + b1_ref[...]
        hid = jnp.square(jnp.maximum(hid, 0.0)).astype(BF16)
        return jnp.dot(hid, w2_ref[...], preferred_element_type=F32)

    _tile_step(n_tiles, None, product, b2_ref, acc_ref, xp_ref, gate_ref, lng_ref, lnb_ref, o_ref,
               alpha=alpha)


def _mlp(h, x, mod3, w1, b1, w2, b2, layer, ln_g, ln_b, *, seq, alpha, tm=512, tf=2048):
    m, d = x.shape
    dff = w1.shape[2]
    tm = min(tm, seq)
    tf = min(tf, dff)
    per_seq = seq // tm
    n_tiles = m // tm
    steps = dff // tf
    rows = tm // steps
    this, prev, step, done = _tile_grid(n_tiles, steps)
    const = pl.BlockSpec((1, d), lambda i, s: (0, 0))
    return pl.pallas_call(
        functools.partial(_mlp_kernel, alpha=alpha, n_tiles=n_tiles),
        grid=(n_tiles + 1, steps),
        in_specs=[pl.BlockSpec((tm, d), lambda i, s: (this(i), 0)),
                  pl.BlockSpec((None, d, tf), lambda i, s: (layer, 0, step(i, s))),
                  pl.BlockSpec((1, tf), lambda i, s: (0, step(i, s))),
                  pl.BlockSpec((None, tf, d), lambda i, s: (layer, step(i, s), 0)),
                  const,
                  pl.BlockSpec((rows, d), lambda i, s: (done(i, s), 0)),
                  _mod_vec(d, per_seq, 5, prev), const, const],
        out_specs=pl.BlockSpec((rows, d), lambda i, s: (done(i, s), 0)),
        out_shape=jax.ShapeDtypeStruct((m, d), F32),
        scratch_shapes=[pltpu.VMEM((2, tm, d), F32)],
        compiler_params=_params(("arbitrary", "arbitrary")),
        name="mlp",
    )(h, w1, b1.reshape(1, dff), w2, b2.reshape(1, d), x, mod3,
      ln_g.reshape(1, d), ln_b.reshape(1, d))


def _conv_residual_ln_kernel(bg_ref, cg_ref, u_ref, cgh_ref, uh_ref, cw_ref, cb_ref, w_ref,
                             xp_ref, gate_ref, lng_ref, lnb_ref, sc_ref, sh_ref,
                             o_ref, h_ref, p_ref, acc_ref, *, alpha, n_tiles, per_seq, tc):
    steps, _, tk = p_ref.shape
    first = (pl.program_id(0) % per_seq) == 0
    s = pl.program_id(1)

    def start():
        for c0 in range(0, steps * tk, tc):
            cs = slice(c0, c0 + tc)
            gated = cg_ref[:, cs].astype(F32) * u_ref[:, cs].astype(F32)
            halo = cgh_ref[:, cs].astype(F32) * uh_ref[:, cs].astype(F32)
            halo = jnp.where(first, 0.0, halo)
            g = jnp.concatenate([halo, gated], axis=0)
            g1 = pltpu.roll(g, 1, 0)[SUBLANES:]
            g2 = pltpu.roll(g, 2, 0)[SUBLANES:]
            conv = (cw_ref[0:1, cs] * g2 + cw_ref[1:2, cs] * g1 + cw_ref[2:3, cs] * gated
                    + cb_ref[:, cs])
            p_ref[c0 // tk, :, c0 % tk:c0 % tk + tc] = (bg_ref[:, cs].astype(F32) * conv).astype(BF16)

    k0 = pl.multiple_of(s * tk, tk)
    product = lambda: jnp.dot(p_ref[s], w_ref[pl.ds(k0, tk), :], preferred_element_type=F32)
    _tile_step(n_tiles, start, product, None, acc_ref, xp_ref, gate_ref, lng_ref, lnb_ref, o_ref,
               (sc_ref, sh_ref, h_ref), alpha=alpha)


def _conv_residual_ln(bcu, conv_w, conv_b, w_out, layer, x, mod3, ln_g, ln_b, *, seq, alpha,
                      tm=512, steps=2, tc=256):
    m, d = x.shape
    tm = min(tm, seq)
    per_seq = seq // tm
    n_tiles = m // tm
    tk, rows = d // steps, tm // steps
    this, prev, step, done = _tile_grid(n_tiles, steps)
    hb = tm // SUBLANES
    const = lambda r: pl.BlockSpec((r, d), lambda i, s: (0, 0))
    tile = lambda col: pl.BlockSpec((tm, d), lambda i, s: (this(i), col))
    halo = lambda col: pl.BlockSpec((SUBLANES, d),
                                    lambda i, s: (jnp.maximum(this(i) * hb - 1, 0), col))
    rows_spec = pl.BlockSpec((rows, d), lambda i, s: (done(i, s), 0))
    return pl.pallas_call(
        functools.partial(_conv_residual_ln_kernel, alpha=alpha, n_tiles=n_tiles, per_seq=per_seq,
                          tc=min(tc, tk)),
        grid=(n_tiles + 1, steps),
        in_specs=[tile(0), tile(1), tile(2), halo(1), halo(2),
                  const(CONV_WIDTH), const(1),
                  pl.BlockSpec((None, d, d), lambda i, s: (layer, 0, 0)),
                  rows_spec,
                  _mod_vec(d, per_seq, 2, prev), const(1), const(1),
                  _mod_vec(d, per_seq, 4, prev), _mod_vec(d, per_seq, 3, prev)],
        out_specs=[rows_spec, rows_spec],
        out_shape=[jax.ShapeDtypeStruct((m, d), F32), jax.ShapeDtypeStruct((m, d), BF16)],
        scratch_shapes=[pltpu.VMEM((steps, tm, tk), BF16), pltpu.VMEM((2, tm, d), F32)],
        compiler_params=_params(("arbitrary", "arbitrary")),
        name="conv_residual_ln",
    )(bcu, bcu, bcu, bcu, bcu, conv_w, conv_b.reshape(1, d), w_out,
      x, mod3, ln_g.reshape(1, d), ln_b.reshape(1, d), mod3, mod3)


def _attention_kernel(q_ref, k_ref, v_ref, o_ref, z_ref, acc_ref, carry_ref, *, t, dh, scale):
    seq, width = q_ref.shape
    heads = width // dh
    row = lax.broadcasted_iota(jnp.int32, (t, t), 0)
    col = lax.broadcasted_iota(jnp.int32, (t, t), 1)
    causal = col < row
    later = jnp.where(row > col, 1.0, 0.0).astype(BF16)

    def rows(i):
        return pl.ds(pl.multiple_of(i * t, t), t)

    hds = [slice(h * dh, (h + 1) * dh) for h in range(heads)]

    def store_logits(qs, j, h):
        z_ref[h] = lax.dot_general(qs[h], k_ref[rows(j), hds[h]], (((1,), (1,)), ((), ())),
                                   preferred_element_type=F32)

    def key_block(qs, j, diagonal):
        log_betas, l1ms, sticks = [], [], []
        for h in range(heads):
            z = z_ref[h]
            neg_abs = lax.bitcast_convert_type(
                lax.bitcast_convert_type(z, jnp.uint32) | jnp.uint32(0x80000000), F32)
            log_beta = jnp.minimum(z, 0.0) - jnp.log(1.0 + jnp.exp2(neg_abs)) * LOG2_E
            l1m = log_beta - z
            if diagonal:
                l1m = jnp.where(causal, l1m, 0.0)
            sticks.append(jnp.dot(l1m.astype(BF16), later, preferred_element_type=F32))
            log_betas.append(log_beta)
            l1ms.append(l1m)
        for h in range(heads):
            store_logits(qs, jnp.maximum(j - 1, 0), h)
        for h, hd in enumerate(hds):
            carry = carry_ref[h]
            a = jnp.exp2(log_betas[h] + sticks[h] + carry)
            if diagonal:
                a = jnp.where(causal, a, 0.0)
            acc_ref[h] += jnp.dot(a.astype(BF16), v_ref[rows(j), hd], preferred_element_type=F32)
            carry_ref[h] = carry + (sticks[h][:, 0:1] + l1ms[h][:, 0:1])

    def q_tile(qi, _):
        qs = [(q_ref[rows(qi), hd].astype(F32) * (scale * LOG2_E)).astype(BF16) for hd in hds]
        acc_ref[...] = jnp.zeros_like(acc_ref)
        carry_ref[...] = jnp.zeros_like(carry_ref)
        for h in range(heads):
            store_logits(qs, qi, h)
        key_block(qs, qi, True)

        def below_diagonal(i, _):
            key_block(qs, qi - 1 - i, False)
            return 0

        lax.fori_loop(0, qi, below_diagonal, 0)
        for h, hd in enumerate(hds):
            o_ref[rows(qi), hd] = acc_ref[h].astype(o_ref.dtype)
        return 0

    lax.fori_loop(0, seq // t, q_tile, 0)


def _attention(qkv, *, bsz, seq, n_heads, t=256, heads_per_step=4):
    m, d3 = qkv.shape
    d = d3 // 3
    dh = d // n_heads
    t = min(t, seq)
    groups = n_heads // heads_per_step
    width = heads_per_step * dh
    return pl.pallas_call(
        functools.partial(_attention_kernel, t=t, dh=dh, scale=dh ** -0.5),
        grid=(bsz, groups),
        in_specs=[pl.BlockSpec((seq, width), lambda b, g: (b, g)),
                  pl.BlockSpec((seq, width), lambda b, g: (b, groups + g)),
                  pl.BlockSpec((seq, width), lambda b, g: (b, 2 * groups + g))],
        out_specs=pl.BlockSpec((seq, width), lambda b, g: (b, g)),
        out_shape=jax.ShapeDtypeStruct((m, d), BF16),
        scratch_shapes=[pltpu.VMEM((heads_per_step, t, t), F32),
                        pltpu.VMEM((heads_per_step, t, dh), F32),
                        pltpu.VMEM((heads_per_step, t, 1), F32)],
        compiler_params=_params(("arbitrary", "arbitrary")),
        name="stickbreak_attention",
    )(qkv, qkv, qkv)


def _trunk(x, c, mod_w, mod_b, ln_g, ln_b, attn_w_qkv, attn_w_o, conv_w_in, conv_w, conv_b,
           conv_w_out, mlp_w1, mlp_b1, mlp_w2, mlp_b2, *, n_heads):
    bsz, seq, d = x.shape
    depth = mod_w.shape[0]
    alpha = (2.0 * depth) ** 0.25
    xf = x.reshape(bsz * seq, d)
    mod = _modulation(c, mod_w, mod_b)
    w_qkv, w_o = _to_bf16(attn_w_qkv), _to_bf16(attn_w_o)
    w_in, w_out = _to_bf16(conv_w_in), _to_bf16(conv_w_out)
    w1, w2 = _to_bf16(mlp_w1), _to_bf16(mlp_w2)
    for i in range(depth):
        mod3 = mod[i].reshape(bsz * N_MOD, 1, d)
        j = i // 2
        if i % 2 == 0:
            qkv = _mod_matmul(xf, mod3, w_qkv, j, seq=seq)
            o = _attention(qkv, bsz=bsz, seq=seq, n_heads=n_heads)
            xf, hf = _proj_residual_ln(o, w_o, j, xf, mod3, ln_g[i, 0], ln_b[i, 0],
                                       seq=seq, alpha=alpha)
        else:
            bcu = _mod_matmul(xf, mod3, w_in, j, seq=seq)
            xf, hf = _conv_residual_ln(bcu, conv_w[j], conv_b[j], w_out, j, xf, mod3,
                                       ln_g[i, 0], ln_b[i, 0], seq=seq, alpha=alpha)
        xf = _mlp(hf, xf, mod3, w1, mlp_b1[i], w2, mlp_b2[i], i, ln_g[i, 1], ln_b[i, 1],
                  seq=seq, alpha=alpha)
    return xf.reshape(bsz, seq, d)


def kernel(x, c, mod_w, mod_b, ln_g, ln_b, attn_w_qkv, attn_w_o, conv_w_in, conv_w, conv_b,
           conv_w_out, mlp_w1, mlp_b1, mlp_w2, mlp_b2):
    return _trunk(x, c, mod_w, mod_b, ln_g, ln_b, attn_w_qkv, attn_w_o, conv_w_in, conv_w, conv_b,
                  conv_w_out, mlp_w1, mlp_b1, mlp_w2, mlp_b2, n_heads=N_HEADS)
```
